```python
import math
import jax, jax.numpy as jnp
from jax import lax
import numpy as np

D_MODEL = 1024
BATCH = 4
SEQ = 4096
DEPTH = 4

CHUNK = 64
N_MIXERS = 2
N_LAYERS_A = (DEPTH + 1) // 2
N_LAYERS_B = DEPTH // 2
FOX_HEADS = 16
FOX_HEAD_DIM = D_MODEL // FOX_HEADS
Q_BLOCK = 128
S5_GROUP = 16
S5_GROUPS = D_MODEL // S5_GROUP
S5_STATE = 64
D_FF = 256 * ((8 * D_MODEL // 3 + 255) // 256)
ALPHA = (2.0 * DEPTH) ** 0.25
BETA = (8.0 * DEPTH) ** -0.25
LN_EPS = 1e-5
NEG_INF = -1e30

kernel_name = "fox_s5_macaron_deepnorm_hybrid"


def layer_norm(x, g, b):
    xf = x.astype(jnp.float32)
    mu = jnp.mean(xf, axis=-1, keepdims=True)
    var = jnp.mean(jnp.square(xf - mu), axis=-1, keepdims=True)
    y = (xf - mu) * lax.rsqrt(var + LN_EPS) * g.astype(jnp.float32) + b.astype(jnp.float32)
    return y.astype(x.dtype)


def swiglu(x, w_in, w_out):
    gate, up = jnp.split(x @ w_in, 2, axis=-1)
    return (jax.nn.silu(gate) * up) @ w_out


def fox_attention(x, w_in, b_f, w_o):
    bsz, seq, d = x.shape
    proj = x @ w_in
    q = proj[..., :d].reshape(bsz, seq, FOX_HEADS, FOX_HEAD_DIM)
    k = proj[..., d:2 * d].reshape(bsz, seq, FOX_HEADS, FOX_HEAD_DIM)
    v = proj[..., 2 * d:3 * d].reshape(bsz, seq, FOX_HEADS, FOX_HEAD_DIM)
    f_logit = proj[..., 3 * d:].astype(jnp.float32) + b_f.astype(jnp.float32)
    log_f = jax.nn.log_sigmoid(f_logit)
    cum = jnp.cumsum(log_f, axis=1).transpose(0, 2, 1)
    n_blk = seq // Q_BLOCK
    q_blocks = q.reshape(bsz, n_blk, Q_BLOCK, FOX_HEADS, FOX_HEAD_DIM).transpose(1, 0, 2, 3, 4)
    c_blocks = cum.reshape(bsz, FOX_HEADS, n_blk, Q_BLOCK).transpose(2, 0, 1, 3)
    starts = jnp.arange(n_blk, dtype=jnp.int32) * Q_BLOCK
    k_pos = jnp.arange(seq, dtype=jnp.int32)
    scale = 1.0 / math.sqrt(FOX_HEAD_DIM)

    def one_block(args):
        q_i, c_i, start = args
        s = jnp.einsum('bqhd,bkhd->bhqk', q_i, k).astype(jnp.float32) * scale
        s = s + (c_i[..., :, None] - cum[:, :, None, :])
        q_pos = start + jnp.arange(Q_BLOCK, dtype=jnp.int32)
        mask = k_pos[None, :] <= q_pos[:, None]
        s = jnp.where(mask, s, NEG_INF)
        p = jax.nn.softmax(s, axis=-1).astype(v.dtype)
        return jnp.einsum('bhqk,bkhd->bqhd', p, v)

    o = lax.map(one_block, (q_blocks, c_blocks, starts))
    o = o.transpose(1, 0, 2, 3, 4).reshape(bsz, seq, d)
    return o @ w_o


def s5_mixer(x, a_re, a_im, log_dt, b_re, b_im, c_re, c_im, d_skip, w_out):
    bsz, seq, d = x.shape
    u = x.reshape(bsz, seq, S5_GROUPS, S5_GROUP).astype(jnp.float32)
    lr = a_re.astype(jnp.float32)
    li = a_im.astype(jnp.float32)
    dt = jnp.exp(log_dt.astype(jnp.float32))[:, None]
    mag = jnp.exp(lr * dt)
    ang = li * dt
    lb_re = mag * jnp.cos(ang)
    lb_im = mag * jnp.sin(ang)
    den = lr * lr + li * li
    nr = lb_re - 1.0
    ni = lb_im
    z_re = (nr * lr + ni * li) / den
    z_im = (ni * lr - nr * li) / den
    br = b_re.astype(jnp.float32)
    bi = b_im.astype(jnp.float32)
    bb_re = z_re[..., None] * br - z_im[..., None] * bi
    bb_im = z_re[..., None] * bi + z_im[..., None] * br
    bu_re = jnp.einsum('bsgh,gph->bsgp', u, bb_re)
    bu_im = jnp.einsum('bsgh,gph->bsgp', u, bb_im)
    a_re_s = jnp.broadcast_to(lb_re[None, None], (1, seq, S5_GROUPS, S5_STATE))
    a_im_s = jnp.broadcast_to(lb_im[None, None], (1, seq, S5_GROUPS, S5_STATE))

    def combine(e1, e2):
        ar1, ai1, hr1, hi1 = e1
        ar2, ai2, hr2, hi2 = e2
        return (ar2 * ar1 - ai2 * ai1,
                ar2 * ai1 + ai2 * ar1,
                ar2 * hr1 - ai2 * hi1 + hr2,
                ar2 * hi1 + ai2 * hr1 + hi2)

    _, _, h_re, h_im = lax.associative_scan(combine, (a_re_s, a_im_s, bu_re, bu_im), axis=1)
    y = (jnp.einsum('ghp,bsgp->bsgh', c_re.astype(jnp.float32), h_re)
         - jnp.einsum('ghp,bsgp->bsgh', c_im.astype(jnp.float32), h_im)
         + d_skip.astype(jnp.float32) * u)
    y = jax.nn.gelu(y.reshape(bsz, seq, d)).astype(x.dtype)
    val, gate = jnp.split(y @ w_out, 2, axis=-1)
    return val * jax.nn.sigmoid(gate)


def setup_inputs(seed: int = 0) -> dict:
    key = jax.random.key(seed)
    ks = jax.random.split(key, 32)
    f32 = jnp.float32
    D, F, H = D_MODEL, D_FF, FOX_HEADS
    G, P, GH = S5_GROUPS, S5_STATE, S5_GROUP

    def nrm(k, shape, fan_in, scale=1.0):
        return jax.random.normal(k, shape, f32) * (scale * fan_in ** -0.5)

    def gain(k, shape):
        return 1.0 + 0.02 * jax.random.normal(k, shape, f32)

    def bias(k, shape):
        return 0.02 * jax.random.normal(k, shape, f32)

    x = jax.random.normal(ks[0], (BATCH, SEQ, D), f32)
    ffn1_w_in = nrm(ks[1], (DEPTH, D, 2 * F), D)
    ffn1_w_out = nrm(ks[2], (DEPTH, F, D), F, BETA)
    ln1_g = gain(ks[3], (DEPTH, D))
    ln1_b = bias(ks[4], (DEPTH, D))
    lnm_g = gain(ks[5], (DEPTH, D))
    lnm_b = bias(ks[6], (DEPTH, D))
    ffn2_w_in = nrm(ks[7], (DEPTH, D, 2 * F), D)
    ffn2_w_out = nrm(ks[8], (DEPTH, F, D), F, BETA)
    ln2_g = gain(ks[9], (DEPTH, D))
    ln2_b = bias(ks[10], (DEPTH, D))

    fox_w_in = jnp.concatenate([
        nrm(ks[11], (N_LAYERS_A, D, 2 * D), D),
        nrm(ks[12], (N_LAYERS_A, D, D), D, BETA),
        nrm(ks[13], (N_LAYERS_A, D, H), D, 0.5),
    ], axis=-1)
    fox_b_f = (jnp.linspace(1.0, 6.0, H, dtype=f32)[None, :]
               + 0.1 * jax.random.normal(ks[14], (N_LAYERS_A, H), f32))
    fox_w_o = nrm(ks[15], (N_LAYERS_A, D, D), D, BETA)

    s5_a_re = -0.5 * jnp.exp(0.05 * jax.random.normal(ks[16], (N_LAYERS_B, G, P), f32))
    s5_a_im = (jnp.pi * jnp.arange(P, dtype=f32))[None, None, :] \
        + 1e-3 * jax.random.normal(ks[17], (N_LAYERS_B, G, P), f32)
    s5_log_dt = jax.random.uniform(ks[18], (N_LAYERS_B, G), f32,
                                   minval=math.log(1e-3), maxval=math.log(1e-1))
    s5_b_re = nrm(ks[19], (N_LAYERS_B, G, P, GH), 2 * GH)
    s5_b_im = nrm(ks[20], (N_LAYERS_B, G, P, GH), 2 * GH)
    s5_c_re = nrm(ks[21], (N_LAYERS_B, G, GH, P), 2 * P)
    s5_c_im = nrm(ks[22], (N_LAYERS_B, G, GH, P), 2 * P)
    s5_d = jax.random.normal(ks[23], (N_LAYERS_B, G, GH), f32)
    s5_w_out = jnp.concatenate([
        nrm(ks[24], (N_LAYERS_B, D, D), D, BETA),
        nrm(ks[25], (N_LAYERS_B, D, D), D),
    ], axis=-1)

    return {"x": x,
            "ffn1_w_in": ffn1_w_in, "ffn1_w_out": ffn1_w_out, "ln1_g": ln1_g, "ln1_b": ln1_b,
            "lnm_g": lnm_g, "lnm_b": lnm_b,
            "ffn2_w_in": ffn2_w_in, "ffn2_w_out": ffn2_w_out, "ln2_g": ln2_g, "ln2_b": ln2_b,
            "fox_w_in": fox_w_in, "fox_b_f": fox_b_f, "fox_w_o": fox_w_o,
            "s5_a_re": s5_a_re, "s5_a_im": s5_a_im, "s5_log_dt": s5_log_dt,
            "s5_b_re": s5_b_re, "s5_b_im": s5_b_im, "s5_c_re": s5_c_re, "s5_c_im": s5_c_im,
            "s5_d": s5_d, "s5_w_out": s5_w_out}


def reference(x, ffn1_w_in, ffn1_w_out, ln1_g, ln1_b, lnm_g, lnm_b,
              ffn2_w_in, ffn2_w_out, ln2_g, ln2_b,
              fox_w_in, fox_b_f, fox_w_o,
              s5_a_re, s5_a_im, s5_log_dt, s5_b_re, s5_b_im, s5_c_re, s5_c_im,
              s5_d, s5_w_out):
    for i in range(DEPTH):
        x = layer_norm(ALPHA * x + 0.5 * swiglu(x, ffn1_w_in[i], ffn1_w_out[i]), ln1_g[i], ln1_b[i])
        j = i // N_MIXERS
        if i % N_MIXERS == 0:
            m = fox_attention(x, fox_w_in[j], fox_b_f[j], fox_w_o[j])
        else:
            m = s5_mixer(x, s5_a_re[j], s5_a_im[j], s5_log_dt[j], s5_b_re[j], s5_b_im[j],
                         s5_c_re[j], s5_c_im[j], s5_d[j], s5_w_out[j])
        x = layer_norm(ALPHA * x + m, lnm_g[i], lnm_b[i])
        x = layer_norm(ALPHA * x + 0.5 * swiglu(x, ffn2_w_in[i], ffn2_w_out[i]), ln2_g[i], ln2_b[i])
    return x
```

```python
import functools

import jax
import jax.numpy as jnp
from jax import lax
from jax.experimental import pallas as pl
from jax.experimental.pallas import tpu as pltpu

F32 = jnp.float32
BF16 = jnp.bfloat16

D_MODEL = 1024
DEPTH = 4
FOX_HEADS = 16
HEAD_DIM = D_MODEL // FOX_HEADS
S5_GROUP = 16
S5_GROUPS = D_MODEL // S5_GROUP
S5_STATE = 64
D_FF = 2816
ALPHA = (2.0 * DEPTH) ** 0.25
LN_EPS = 1e-5
NEG_INF = -1e30

LANES = 128
MXU_DIM = 256
VMEM_LIMIT = 56 * 1024 * 1024

FFN_TM = 512
FFN_FC = 256
TOK_T = 512
ATT_T = 512
CH = 16
S5_KT = MXU_DIM
S5_NKT = D_MODEL // S5_KT
S5_SL = (S5_KT // S5_GROUP) * S5_STATE
C_PARTS = 3
HEAD_SLOT_SHIFT = 3
HEAD_SLOT = 1 << HEAD_SLOT_SHIFT


def _params(n_grid):
    return pltpu.CompilerParams(dimension_semantics=("arbitrary",) * n_grid,
                                vmem_limit_bytes=VMEM_LIMIT)


def _const_spec(shape):
    nd = len(shape)
    return pl.BlockSpec(shape, lambda *_: (0,) * nd, pipeline_mode=pl.Buffered(1))


def _ln(y, g, b):
    mu = jnp.mean(y, axis=-1, keepdims=True)
    d = y - mu
    var = jnp.mean(d * d, axis=-1, keepdims=True)
    return d * lax.rsqrt(var + LN_EPS) * g + b


def _split3(c):
    hi = c.astype(BF16)
    r1 = c - hi.astype(F32)
    mid = r1.astype(BF16)
    lo = (r1 - mid.astype(F32)).astype(BF16)
    return hi, mid, lo


def _ffn_ln_kernel(x_ref, win_ref, wout_ref, g_ref, b_ref, o_ref, h_ref):
    x = x_ref[...]
    xb = x.astype(BF16)
    for c in range(D_FF // FFN_FC):
        lo = c * FFN_FC
        gate = jnp.dot(xb, win_ref[:, lo:lo + FFN_FC], preferred_element_type=F32)
        up = jnp.dot(xb, win_ref[:, D_FF + lo:D_FF + lo + FFN_FC], preferred_element_type=F32)
        h_ref[:, lo:lo + FFN_FC] = (gate * jax.nn.sigmoid(gate) * up).astype(BF16)
    acc = jnp.dot(h_ref[...], wout_ref[...], preferred_element_type=F32)
    o_ref[...] = _ln(ALPHA * x + 0.5 * acc, g_ref[...], b_ref[...])


def _ffn_ln(x2, win, wout, g, b):
    n, d = x2.shape
    tm = min(FFN_TM, n)
    return pl.pallas_call(
        _ffn_ln_kernel,
        grid=(n // tm,),
        in_specs=[pl.BlockSpec((tm, d), lambda i: (i, 0)),
                  _const_spec(win.shape), _const_spec(wout.shape),
                  _const_spec(g.shape), _const_spec(b.shape)],
        out_specs=pl.BlockSpec((tm, d), lambda i: (i, 0)),
        out_shape=jax.ShapeDtypeStruct((n, d), F32),
        scratch_shapes=[pltpu.VMEM((tm, D_FF), BF16)],
        compiler_params=_params(1),
        name="ffn_ln",
    )(x2, win, wout, g, b)


def _fox_proj_kernel(x_ref, wqkv_ref, wf_ref, bf_ref, eq_ref, ek_ref, oq_ref, ok_ref,
                     q_ref, k_ref, v_ref, cq_ref, ck_ref, carry_ref):
    @pl.when(pl.program_id(1) == 0)
    def _():
        carry_ref[...] = jnp.zeros_like(carry_ref)

    xb = x_ref[0].astype(BF16)
    t = xb.shape[0]
    d = D_MODEL
    q_ref[0] = jnp.dot(xb, wqkv_ref[:, 0:d], preferred_element_type=F32).astype(BF16)
    k_ref[0] = jnp.dot(xb, wqkv_ref[:, d:2 * d], preferred_element_type=F32).astype(BF16)
    v_ref[0] = jnp.dot(xb, wqkv_ref[:, 2 * d:3 * d], preferred_element_type=F32).astype(BF16)

    f = jnp.dot(xb, wf_ref[...], preferred_element_type=F32) + bf_ref[...]
    logf = jnp.minimum(f, 0.0) - jnp.log1p(jnp.exp(-jnp.abs(f)))
    row = lax.broadcasted_iota(jnp.int32, (t, t), 0)
    col = lax.broadcasted_iota(jnp.int32, (t, t), 1)
    tri = (col <= row).astype(BF16)
    pieces = jnp.concatenate(_split3(logf), axis=1)
    cum3 = jnp.dot(tri, pieces, preferred_element_type=F32)
    c = (cum3[:, 0:LANES] + cum3[:, LANES:2 * LANES] + cum3[:, 2 * LANES:3 * LANES]
         + carry_ref[...])
    carry_ref[...] = c[t - 1:t, :]
    ccat = jnp.concatenate(_split3(c), axis=1)
    cq_ref[0] = (jnp.dot(ccat, eq_ref[...], preferred_element_type=F32) + oq_ref[...]).astype(BF16)
    ck_ref[0] = (jnp.dot(ccat, ek_ref[...], preferred_element_type=F32) + ok_ref[...]).astype(BF16)


def _bias_routing():
    eq = jnp.zeros((C_PARTS * LANES, LANES), F32)
    ek = jnp.zeros((C_PARTS * LANES, LANES), F32)
    oq = jnp.zeros((1, LANES), F32)
    ok = jnp.zeros((1, LANES), F32)
    h = jnp.arange(FOX_HEADS)
    for p in range(C_PARTS):
        eq = eq.at[p * LANES + h, h * HEAD_SLOT + p].set(1.0)
        ek = ek.at[p * LANES + h, h * HEAD_SLOT + C_PARTS + p].set(-1.0)
        oq = oq.at[0, h * HEAD_SLOT + C_PARTS + p].set(1.0)
        ok = ok.at[0, h * HEAD_SLOT + p].set(1.0)
    return eq.astype(BF16), ek.astype(BF16), oq, ok


def _fox_proj(x3, wqkv, wf, bf):
    bsz, seq, d = x3.shape
    t = min(TOK_T, seq)
    eq, ek, oq, ok = _bias_routing()
    tok = lambda w: pl.BlockSpec((1, t, w), lambda b, i: (b, i, 0))
    big = jax.ShapeDtypeStruct((bsz, seq, d), BF16)
    small = jax.ShapeDtypeStruct((bsz, seq, LANES), BF16)
    return pl.pallas_call(
        _fox_proj_kernel,
        grid=(bsz, seq // t),
        in_specs=[tok(d), _const_spec(wqkv.shape), _const_spec(wf.shape), _const_spec(bf.shape),
                  _const_spec(eq.shape), _const_spec(ek.shape), _const_spec(oq.shape),
                  _const_spec(ok.shape)],
        out_specs=[tok(d), tok(d), tok(d), tok(LANES), tok(LANES)],
        out_shape=[big, big, big, small, small],
        scratch_shapes=[pltpu.VMEM((1, LANES), F32)],
        compiler_params=_params(2),
        name="fox_proj",
    )(x3, wqkv, wf, bf, eq, ek, oq, ok)


def _fox_attn_kernel(q_ref, cq_ref, k_ref, ck_ref, v_ref, o_ref, m_ref, l_ref, acc_ref):
    pair = pl.program_id(1)
    qi = pl.program_id(2)
    qp = q_ref[0]
    cq = cq_ref[0]
    t = qp.shape[0]
    lane = lax.broadcasted_iota(jnp.int32, (t, LANES), 1)
    row = lax.broadcasted_iota(jnp.int32, (t, t), 0)
    col = lax.broadcasted_iota(jnp.int32, (t, t), 1)
    first_half = lane < HEAD_DIM
    outs = []
    for hh in range(2):
        head = 2 * pair + hh
        qm = jnp.where(first_half if hh == 0 else jnp.logical_not(first_half), qp, jnp.zeros_like(qp))
        cm = jnp.where((lane >> HEAD_SLOT_SHIFT) == head, cq, jnp.zeros_like(cq))
        lhs = jnp.concatenate([qm, cm], axis=1)
        m_ref[...] = jnp.full(m_ref.shape, NEG_INF, F32)
        l_ref[...] = jnp.zeros(l_ref.shape, F32)
        acc_ref[...] = jnp.zeros(acc_ref.shape, F32)

        def step(j, masked, lhs=lhs):
            off = pl.multiple_of(j * t, t)
            rhs = jnp.concatenate([k_ref[0, pl.ds(off, t), :], ck_ref[0, pl.ds(off, t), :]], axis=1)
            s = lax.dot_general(lhs, rhs, (((1,), (1,)), ((), ())), preferred_element_type=F32)
            if masked:
                s = jnp.where(col <= row, s, NEG_INF)
            m_prev = m_ref[...]
            m_new = jnp.maximum(m_prev, jnp.max(s, axis=1, keepdims=True))
            p = jnp.exp(s - m_new)
            a = jnp.exp(m_prev - m_new)
            l_ref[...] = a * l_ref[...] + jnp.sum(p, axis=1, keepdims=True)
            acc_ref[...] = a * acc_ref[...] + jnp.dot(
                p.astype(BF16), v_ref[0, pl.ds(off, t), :], preferred_element_type=F32)
            m_ref[...] = m_new

        def body(j, carry):
            step(j, False)
            return carry

        lax.fori_loop(0, qi, body, 0)
        step(qi, True)
        outs.append(acc_ref[...] / l_ref[...])
    o_ref[0] = jnp.where(first_half, outs[0], outs[1]).astype(BF16)


def _fox_attn(q, cq, k, ck, v):
    bsz, seq, d = q.shape
    t = min(ATT_T, seq)
    return pl.pallas_call(
        _fox_attn_kernel,
        grid=(bsz, d // LANES, seq // t),
        in_specs=[pl.BlockSpec((1, t, LANES), lambda b, p, i: (b, i, p)),
                  pl.BlockSpec((1, t, LANES), lambda b, p, i: (b, i, 0)),
                  pl.BlockSpec((1, seq, LANES), lambda b, p, i: (b, 0, p)),
                  pl.BlockSpec((1, seq, LANES), lambda b, p, i: (b, 0, 0)),
                  pl.BlockSpec((1, seq, LANES), lambda b, p, i: (b, 0, p))],
        out_specs=pl.BlockSpec((1, t, LANES), lambda b, p, i: (b, i, p)),
        out_shape=jax.ShapeDtypeStruct((bsz, seq, d), BF16),
        scratch_shapes=[pltpu.VMEM((t, 1), F32), pltpu.VMEM((t, 1), F32),
                        pltpu.VMEM((t, LANES), F32)],
        compiler_params=_params(3),
        name="fox_attn",
    )(q, cq, k, ck, v)


def _proj_ln_kernel(x_ref, a_ref, w_ref, g_ref, b_ref, o_ref):
    m = jnp.dot(a_ref[...], w_ref[...], preferred_element_type=F32)
    o_ref[...] = _ln(ALPHA * x_ref[...] + m, g_ref[...], b_ref[...])


def _proj_ln(x2, a2, w, g, b):
    n, d = x2.shape
    tm = min(TOK_T, n)
    tile = pl.BlockSpec((tm, d), lambda i: (i, 0))
    return pl.pallas_call(
        _proj_ln_kernel,
        grid=(n // tm,),
        in_specs=[tile, tile, _const_spec(w.shape), _const_spec(g.shape), _const_spec(b.shape)],
        out_specs=tile,
        out_shape=jax.ShapeDtypeStruct((n, d), F32),
        compiler_params=_params(1),
        name="proj_ln",
    )(x2, a2, w, g, b)


def _cmul(ar, ai, br, bi):
    return ar * br - ai * bi, ar * bi + ai * br


def _s5_scan_block(sr, si, lam_r, lam_i, cr, ci, pr_ref, pi_ref, r):
    rows = lambda i: pl.ds(i, r, stride=CH)
    hr = sr[rows(0), :]
    hi = si[rows(0), :]
    for i in range(1, CH):
        mr, mi = _cmul(lam_r, lam_i, hr, hi)
        hr = mr + sr[rows(i), :]
        hi = mi + si[rows(i), :]
        sr[rows(i), :] = hr
        si[rows(i), :] = hi
    pows = [(lam_r, lam_i)]
    for i in range(1, CH):
        pows.append(_cmul(lam_r, lam_i, *pows[-1]))
    lcr, lci = pows[CH - 1]
    for c in range(r):
        pr_ref[c:c + 1, :] = cr
        pi_ref[c:c + 1, :] = ci
        mr, mi = _cmul(lcr, lci, cr, ci)
        cr = mr + hr[c:c + 1, :]
        ci = mi + hi[c:c + 1, :]
    inr = pr_ref[...]
    ini = pi_ref[...]
    for i in range(CH):
        mr, mi = _cmul(pows[i][0], pows[i][1], inr, ini)
        sr[rows(i), :] = sr[rows(i), :] + mr
        si[rows(i), :] = si[rows(i), :] + mi
    return cr, ci


def _s5_ln_kernel(x_ref, wb_ref, wc_ref, lam_ref, d_ref, wout_ref, g_ref, b_ref, o_ref,
                  sr_ref, si_ref, pr_ref, pi_ref, y_ref, carry_ref):
    @pl.when(pl.program_id(1) == 0)
    def _():
        carry_ref[...] = jnp.zeros_like(carry_ref)

    x = x_ref[0]
    xb = x.astype(BF16)
    t = x.shape[0]
    r = t // CH
    nb = S5_SL // LANES
    for kt in range(S5_NKT):
        cols = slice(kt * S5_KT, (kt + 1) * S5_KT)
        bu = jnp.dot(xb[:, cols], wb_ref[kt], preferred_element_type=F32)
        for j in range(nb):
            sr_ref[j] = bu[:, j * LANES:(j + 1) * LANES]
            si_ref[j] = bu[:, S5_SL + j * LANES:S5_SL + (j + 1) * LANES]

        def lane_block(j, carry, kt=kt):
            re_row = 2 * kt * nb + j
            im_row = re_row + nb
            cr, ci = _s5_scan_block(
                sr_ref.at[j], si_ref.at[j],
                lam_ref[pl.ds(re_row, 1), :], lam_ref[pl.ds(im_row, 1), :],
                carry_ref[pl.ds(re_row, 1), :], carry_ref[pl.ds(im_row, 1), :],
                pr_ref, pi_ref, r)
            carry_ref[pl.ds(re_row, 1), :] = cr
            carry_ref[pl.ds(im_row, 1), :] = ci
            return carry

        lax.fori_loop(0, nb, lane_block, 0)
        hcat = jnp.concatenate([sr_ref[j].astype(BF16) for j in range(nb)]
                               + [si_ref[j].astype(BF16) for j in range(nb)], axis=1)
        y_ref[:, cols] = (jnp.dot(hcat, wc_ref[kt], preferred_element_type=F32)
                          + d_ref[:, cols] * x[:, cols])
    ge = jax.nn.gelu(y_ref[...]).astype(BF16)
    vg = jnp.dot(ge, wout_ref[...], preferred_element_type=F32)
    m = vg[:, :D_MODEL] * jax.nn.sigmoid(vg[:, D_MODEL:])
    o_ref[0] = _ln(ALPHA * x + m, g_ref[...], b_ref[...])


def _s5_tables(a_re, a_im, log_dt, b_re, b_im, c_re, c_im, d_skip):
    lr = a_re.astype(F32)
    li = a_im.astype(F32)
    dt = jnp.exp(log_dt.astype(F32))[:, None]
    mag = jnp.exp(lr * dt)
    ang = li * dt
    lb_re = mag * jnp.cos(ang)
    lb_im = mag * jnp.sin(ang)
    den = lr * lr + li * li
    nr = lb_re - 1.0
    ni = lb_im
    z_re = (nr * lr + ni * li) / den
    z_im = (ni * lr - nr * li) / den
    br = b_re.astype(F32)
    bi = b_im.astype(F32)
    bb_re = z_re[..., None] * br - z_im[..., None] * bi
    bb_im = z_re[..., None] * bi + z_im[..., None] * br
    gk = S5_KT // S5_GROUP
    eye = jnp.eye(gk, dtype=F32)

    def blockdiag_b(m):
        m = m.reshape(S5_NKT, gk, S5_STATE, S5_GROUP)
        w = jnp.einsum('kgph,gj->kghjp', m, eye)
        return w.reshape(S5_NKT, S5_KT, S5_SL)

    def blockdiag_c(m):
        m = m.reshape(S5_NKT, gk, S5_GROUP, S5_STATE)
        w = jnp.einsum('kghp,gj->kgpjh', m, eye)
        return w.reshape(S5_NKT, S5_SL, S5_KT)

    wb = jnp.concatenate([blockdiag_b(bb_re), blockdiag_b(bb_im)], axis=2).astype(BF16)
    wc = jnp.concatenate([blockdiag_c(c_re.astype(F32)), -blockdiag_c(c_im.astype(F32))],
                         axis=1).astype(BF16)
    lam = jnp.stack([lb_re.reshape(S5_NKT, S5_SL), lb_im.reshape(S5_NKT, S5_SL)], axis=1)
    lam = lam.reshape(2 * S5_NKT * (S5_SL // LANES), LANES)
    return wb, wc, lam, d_skip.astype(F32).reshape(1, D_MODEL)


def _s5_ln(x3, wb, wc, lam, dsk, wout, g, b):
    bsz, seq, d = x3.shape
    t = min(TOK_T, seq)
    r = t // CH
    tok = pl.BlockSpec((1, t, d), lambda bi, i: (bi, i, 0))
    return pl.pallas_call(
        _s5_ln_kernel,
        grid=(bsz, seq // t),
        in_specs=[tok, _const_spec(wb.shape), _const_spec(wc.shape), _const_spec(lam.shape),
                  _const_spec(dsk.shape), _const_spec(wout.shape), _const_spec(g.shape),
                  _const_spec(b.shape)],
        out_specs=tok,
        out_shape=jax.ShapeDtypeStruct((bsz, seq, d), F32),
        scratch_shapes=[pltpu.VMEM((S5_SL // LANES, t, LANES), F32),
                        pltpu.VMEM((S5_SL // LANES, t, LANES), F32),
                        pltpu.VMEM((r, LANES), F32), pltpu.VMEM((r, LANES), F32),
                        pltpu.VMEM((t, d), F32), pltpu.VMEM(lam.shape, F32)],
        compiler_params=_params(2),
        name="s5_ln",
    )(x3, wb, wc, lam, dsk, wout, g, b)


def kernel(x, ffn1_w_in, ffn1_w_out, ln1_g, ln1_b, lnm_g, lnm_b, ffn2_w_in, ffn2_w_out, ln2_g, ln2_b, fox_w_in, fox_b_f, fox_w_o, s5_a_re, s5_a_im, s5_log_dt, s5_b_re, s5_b_im, s5_c_re, s5_c_im, s5_d, s5_w_out):
    bsz, seq, d = x.shape
    n = bsz * seq
    row = lambda v: v.astype(F32).reshape(1, d)
    x2 = x.reshape(n, d)
    for i in range(DEPTH):
        x2 = _ffn_ln(x2, ffn1_w_in[i].astype(BF16), ffn1_w_out[i].astype(BF16),
                     row(ln1_g[i]), row(ln1_b[i]))
        j = i // 2
        if i % 2 == 0:
            w = fox_w_in[j]
            scale = 1.0 / (HEAD_DIM ** 0.5)
            wqkv = jnp.concatenate([w[:, :d] * scale, w[:, d:3 * d]], axis=1).astype(BF16)
            wf = jnp.pad(w[:, 3 * d:], ((0, 0), (0, LANES - FOX_HEADS))).astype(BF16)
            bf = jnp.pad(fox_b_f[j].astype(F32), (0, LANES - FOX_HEADS)).reshape(1, LANES)
            q, k, v, cq, ck = _fox_proj(x2.reshape(bsz, seq, d), wqkv, wf, bf)
            a = _fox_attn(q, cq, k, ck, v)
            x2 = _proj_ln(x2, a.reshape(n, d), fox_w_o[j].astype(BF16), row(lnm_g[i]), row(lnm_b[i]))
        else:
            wb, wc, lam, dsk = _s5_tables(s5_a_re[j], s5_a_im[j], s5_log_dt[j], s5_b_re[j],
                                          s5_b_im[j], s5_c_re[j], s5_c_im[j], s5_d[j])
            x2 = _s5_ln(x2.reshape(bsz, seq, d), wb, wc, lam, dsk, s5_w_out[j].astype(BF16),
                        row(lnm_g[i]), row(lnm_b[i])).reshape(n, d)
        x2 = _ffn_ln(x2, ffn2_w_in[i].astype(BF16), ffn2_w_out[i].astype(BF16),
                     row(ln2_g[i]), row(ln2_b[i]))
    return x2.reshape(bsz, seq, d)
```

```python
import functools

import jax
import jax.numpy as jnp
from jax import lax
from jax.experimental import pallas as pl
from jax.experimental.pallas import tpu as pltpu

F32 = jnp.float32
BF16 = jnp.bfloat16

D_MODEL = 1024
DEPTH = 4
FOX_HEADS = 16
HEAD_DIM = D_MODEL // FOX_HEADS
S5_GROUP = 16
S5_GROUPS = D_MODEL // S5_GROUP
S5_STATE = 64
D_FF = 2816
ALPHA = (2.0 * DEPTH) ** 0.25
LN_EPS = 1e-5
NEG_INF = -1e30

LANES = 128
MXU_DIM = 256
VMEM_LIMIT = 56 * 1024 * 1024

FFN_TM = 512
FFN_FC = 256
TOK_T = 512
ATT_T = 512
CH = 16
S5_KT = MXU_DIM
S5_NKT = D_MODEL // S5_KT
S5_SL = (S5_KT // S5_GROUP) * S5_STATE
C_PARTS = 3
DEN_ROWS = 16
LOG2E = 1.4426950408889634
HEAD_SLOT_SHIFT = 3
HEAD_SLOT = 1 << HEAD_SLOT_SHIFT


def _params(n_grid):
    return pltpu.CompilerParams(dimension_semantics=("arbitrary",) * n_grid,
                                vmem_limit_bytes=VMEM_LIMIT)


def _const_spec(shape):
    nd = len(shape)
    return pl.BlockSpec(shape, lambda *_: (0,) * nd, pipeline_mode=pl.Buffered(1))


def _ln(y, g, b):
    mu = jnp.mean(y, axis=-1, keepdims=True)
    d = y - mu
    var = jnp.mean(d * d, axis=-1, keepdims=True)
    return d * lax.rsqrt(var + LN_EPS) * g + b


def _split3(c):
    hi = c.astype(BF16)
    r1 = c - hi.astype(F32)
    mid = r1.astype(BF16)
    lo = (r1 - mid.astype(F32)).astype(BF16)
    return hi, mid, lo


def _ffn_ln_kernel(x_ref, win_ref, wout_ref, g_ref, b_ref, o_ref, h_ref):
    x = x_ref[...]
    xb = x.astype(BF16)
    for c in range(D_FF // FFN_FC):
        lo = c * FFN_FC
        gate = jnp.dot(xb, win_ref[:, lo:lo + FFN_FC], preferred_element_type=F32)
        up = jnp.dot(xb, win_ref[:, D_FF + lo:D_FF + lo + FFN_FC], preferred_element_type=F32)
        h_ref[:, lo:lo + FFN_FC] = (gate * jax.nn.sigmoid(gate) * up).astype(BF16)
    acc = jnp.dot(h_ref[...], wout_ref[...], preferred_element_type=F32)
    o_ref[...] = _ln(ALPHA * x + 0.5 * acc, g_ref[...], b_ref[...])


def _ffn_ln(x2, win, wout, g, b):
    n, d = x2.shape
    tm = min(FFN_TM, n)
    return pl.pallas_call(
        _ffn_ln_kernel,
        grid=(n // tm,),
        in_specs=[pl.BlockSpec((tm, d), lambda i: (i, 0)),
                  _const_spec(win.shape), _const_spec(wout.shape),
                  _const_spec(g.shape), _const_spec(b.shape)],
        out_specs=pl.BlockSpec((tm, d), lambda i: (i, 0)),
        out_shape=jax.ShapeDtypeStruct((n, d), F32),
        scratch_shapes=[pltpu.VMEM((tm, D_FF), BF16)],
        compiler_params=_params(1),
        name="ffn_ln",
    )(x2, win, wout, g, b)


def _fox_proj_kernel(x_ref, wqk_ref, wvt_ref, wf_ref, bf_ref, eq_ref, ek_ref, oq_ref, ok_ref,
                     q_ref, k_ref, vt_ref, cq_ref, ck_ref, carry_ref):
    @pl.when(pl.program_id(1) == 0)
    def _():
        carry_ref[...] = jnp.zeros_like(carry_ref)

    xb = x_ref[0].astype(BF16)
    t = xb.shape[0]
    d = D_MODEL
    q_ref[0] = jnp.dot(xb, wqk_ref[:, 0:d], preferred_element_type=F32).astype(BF16)
    k_ref[0] = jnp.dot(xb, wqk_ref[:, d:2 * d], preferred_element_type=F32).astype(BF16)
    vt_ref[0, 0] = lax.dot_general(wvt_ref[...], xb, (((1,), (1,)), ((), ())),
                                   preferred_element_type=F32).astype(BF16)

    f = jnp.dot(xb, wf_ref[...], preferred_element_type=F32) + bf_ref[...]
    logf = jnp.minimum(f, 0.0) - jnp.log1p(jnp.exp(-jnp.abs(f)))
    row = lax.broadcasted_iota(jnp.int32, (t, t), 0)
    col = lax.broadcasted_iota(jnp.int32, (t, t), 1)
    tri = (col <= row).astype(BF16)
    pieces = jnp.concatenate(_split3(logf), axis=1)
    cum3 = jnp.dot(tri, pieces, preferred_element_type=F32)
    c = (cum3[:, 0:LANES] + cum3[:, LANES:2 * LANES] + cum3[:, 2 * LANES:3 * LANES]
         + carry_ref[...])
    carry_ref[...] = c[t - 1:t, :]
    ccat = jnp.concatenate(_split3(c * LOG2E), axis=1)
    cq_ref[0] = (jnp.dot(ccat, eq_ref[...], preferred_element_type=F32) + oq_ref[...]).astype(BF16)
    ck_ref[0] = (jnp.dot(ccat, ek_ref[...], preferred_element_type=F32) + ok_ref[...]).astype(BF16)


def _bias_routing():
    eq = jnp.zeros((C_PARTS * LANES, LANES), F32)
    ek = jnp.zeros((C_PARTS * LANES, LANES), F32)
    oq = jnp.zeros((1, LANES), F32)
    ok = jnp.zeros((1, LANES), F32)
    h = jnp.arange(FOX_HEADS)
    for p in range(C_PARTS):
        eq = eq.at[p * LANES + h, h * HEAD_SLOT + p].set(1.0)
        ek = ek.at[p * LANES + h, h * HEAD_SLOT + C_PARTS + p].set(-1.0)
        oq = oq.at[0, h * HEAD_SLOT + C_PARTS + p].set(1.0)
        ok = ok.at[0, h * HEAD_SLOT + p].set(1.0)
    return eq.astype(BF16), ek.astype(BF16), oq, ok


def _fox_proj(x3, wqk, wvt, wf, bf):
    bsz, seq, d = x3.shape
    t = min(ATT_T, seq)
    eq, ek, oq, ok = _bias_routing()
    tok = lambda w: pl.BlockSpec((1, t, w), lambda b, i: (b, i, 0))
    big = jax.ShapeDtypeStruct((bsz, seq, d), BF16)
    small = jax.ShapeDtypeStruct((bsz, seq, LANES), BF16)
    return pl.pallas_call(
        _fox_proj_kernel,
        grid=(bsz, seq // t),
        in_specs=[tok(d), _const_spec(wqk.shape), _const_spec(wvt.shape), _const_spec(wf.shape),
                  _const_spec(bf.shape), _const_spec(eq.shape), _const_spec(ek.shape),
                  _const_spec(oq.shape), _const_spec(ok.shape)],
        out_specs=[tok(d), tok(d), pl.BlockSpec((1, 1, d, t), lambda b, i: (b, i, 0, 0)),
                   tok(LANES), tok(LANES)],
        out_shape=[big, big, jax.ShapeDtypeStruct((bsz, seq // t, d, t), BF16), small, small],
        scratch_shapes=[pltpu.VMEM((1, LANES), F32)],
        compiler_params=_params(2),
        name="fox_proj",
    )(x3, wqk, wvt, wf, bf, eq, ek, oq, ok)


def _fox_attn_kernel(q_ref, cq_ref, k_ref, ck_ref, vt_ref, o_ref,
                     m_ref, acc_ref, sa_ref, sb_ref):
    pair = pl.program_id(1)
    qi = pl.program_id(2)
    qp = q_ref[0]
    cq = cq_ref[0]
    t = qp.shape[0]
    lane = lax.broadcasted_iota(jnp.int32, (t, LANES), 1)
    key_pos = lax.broadcasted_iota(jnp.int32, (t, t), 0)
    qry_pos = lax.broadcasted_iota(jnp.int32, (t, t), 1)
    first_half = lane < HEAD_DIM
    lhs = []
    for hh in range(2):
        head = 2 * pair + hh
        qm = jnp.where(first_half if hh == 0 else jnp.logical_not(first_half), qp, jnp.zeros_like(qp))
        cm = jnp.where((lane >> HEAD_SLOT_SHIFT) == head, cq, jnp.zeros_like(cq))
        lhs.append(jnp.concatenate([qm, cm], axis=1))
    m_ref[...] = jnp.full(m_ref.shape, NEG_INF, F32)
    acc_ref[...] = jnp.zeros(acc_ref.shape, F32)
    ones_rows = jnp.ones((DEN_ROWS, t), BF16)

    def scores(j, s_ref):
        off = pl.multiple_of(j * t, t)
        rhs = jnp.concatenate([k_ref[0, pl.ds(off, t), :], ck_ref[0, pl.ds(off, t), :]], axis=1)
        for hh in range(2):
            s_ref[hh] = lax.dot_general(rhs, lhs[hh], (((1,), (1,)), ((), ())),
                                        preferred_element_type=F32)

    def absorb(j, s_ref, masked):
        for hh in range(2):
            st = s_ref[hh]
            if masked:
                st = jnp.where(key_pos <= qry_pos, st, NEG_INF)
            m_prev = m_ref[hh]
            m_new = jnp.maximum(m_prev, jnp.max(st, axis=0, keepdims=True))
            pt = jnp.exp2(st - m_new)
            a = jnp.exp2(m_prev - m_new)
            vt = jnp.concatenate([vt_ref[0, j, hh * HEAD_DIM:(hh + 1) * HEAD_DIM, :], ones_rows],
                                 axis=0)
            acc_ref[hh] = a * acc_ref[hh] + jnp.dot(vt, pt.astype(BF16),
                                                    preferred_element_type=F32)
            m_ref[hh] = m_new

    scores(0, sa_ref)

    def body(i, carry):
        j = 2 * i
        scores(j + 1, sb_ref)
        absorb(j, sa_ref, False)
        scores(j + 2, sa_ref)
        absorb(j + 1, sb_ref, False)
        return carry

    lax.fori_loop(0, qi // 2, body, 0)

    @pl.when(qi % 2 == 0)
    def _():
        absorb(qi, sa_ref, True)

    @pl.when(qi % 2 == 1)
    def _():
        scores(qi, sb_ref)
        absorb(qi - 1, sa_ref, False)
        absorb(qi, sb_ref, True)

    ot = jnp.concatenate([acc_ref[hh, :HEAD_DIM, :] / acc_ref[hh, HEAD_DIM:HEAD_DIM + 1, :]
                          for hh in range(2)], axis=0)
    o_ref[0] = ot.T.astype(BF16)


def _fox_attn(q, cq, k, ck, vt):
    bsz, seq, d = q.shape
    t = min(ATT_T, seq)
    return pl.pallas_call(
        _fox_attn_kernel,
        grid=(bsz, d // LANES, seq // t),
        in_specs=[pl.BlockSpec((1, t, LANES), lambda b, p, i: (b, i, p)),
                  pl.BlockSpec((1, t, LANES), lambda b, p, i: (b, i, 0)),
                  pl.BlockSpec((1, seq, LANES), lambda b, p, i: (b, 0, p)),
                  pl.BlockSpec((1, seq, LANES), lambda b, p, i: (b, 0, 0)),
                  pl.BlockSpec((1, seq // t, LANES, t), lambda b, p, i: (b, 0, p, 0))],
        out_specs=pl.BlockSpec((1, t, LANES), lambda b, p, i: (b, i, p)),
        out_shape=jax.ShapeDtypeStruct((bsz, seq, d), BF16),
        scratch_shapes=[pltpu.VMEM((2, 1, t), F32),
                        pltpu.VMEM((2, HEAD_DIM + DEN_ROWS, t), F32),
                        pltpu.VMEM((2, t, t), F32), pltpu.VMEM((2, t, t), F32)],
        compiler_params=_params(3),
        name="fox_attn",
    )(q, cq, k, ck, vt)


def _proj_ln_kernel(x_ref, a_ref, w_ref, g_ref, b_ref, o_ref):
    m = jnp.dot(a_ref[...], w_ref[...], preferred_element_type=F32)
    o_ref[...] = _ln(ALPHA * x_ref[...] + m, g_ref[...], b_ref[...])


def _proj_ln(x2, a2, w, g, b):
    n, d = x2.shape
    tm = min(TOK_T, n)
    tile = pl.BlockSpec((tm, d), lambda i: (i, 0))
    return pl.pallas_call(
        _proj_ln_kernel,
        grid=(n // tm,),
        in_specs=[tile, tile, _const_spec(w.shape), _const_spec(g.shape), _const_spec(b.shape)],
        out_specs=tile,
        out_shape=jax.ShapeDtypeStruct((n, d), F32),
        compiler_params=_params(1),
        name="proj_ln",
    )(x2, a2, w, g, b)


def _cmul(ar, ai, br, bi):
    return ar * br - ai * bi, ar * bi + ai * br


def _s5_scan_block(sr, si, lam_r, lam_i, cr, ci, pr_ref, pi_ref, r):
    rows = lambda i: pl.ds(i, r, stride=CH)
    hr = sr[rows(0), :]
    hi = si[rows(0), :]
    for i in range(1, CH):
        mr, mi = _cmul(lam_r, lam_i, hr, hi)
        hr = mr + sr[rows(i), :]
        hi = mi + si[rows(i), :]
        sr[rows(i), :] = hr
        si[rows(i), :] = hi
    pows = [(lam_r, lam_i)]
    for i in range(1, CH):
        pows.append(_cmul(lam_r, lam_i, *pows[-1]))
    lcr, lci = pows[CH - 1]
    for c in range(r):
        pr_ref[c:c + 1, :] = cr
        pi_ref[c:c + 1, :] = ci
        mr, mi = _cmul(lcr, lci, cr, ci)
        cr = mr + hr[c:c + 1, :]
        ci = mi + hi[c:c + 1, :]
    inr = pr_ref[...]
    ini = pi_ref[...]
    for i in range(CH):
        mr, mi = _cmul(pows[i][0], pows[i][1], inr, ini)
        sr[rows(i), :] = sr[rows(i), :] + mr
        si[rows(i), :] = si[rows(i), :] + mi
    return cr, ci


def _s5_ln_kernel(x_ref, wb_ref, wc_ref, lam_ref, d_ref, wout_ref, g_ref, b_ref, o_ref,
                  sr_ref, si_ref, pr_ref, pi_ref, y_ref, carry_ref):
    @pl.when(pl.program_id(1) == 0)
    def _():
        carry_ref[...] = jnp.zeros_like(carry_ref)

    x = x_ref[0]
    xb = x.astype(BF16)
    t = x.shape[0]
    r = t // CH
    nb = S5_SL // LANES
    for kt in range(S5_NKT):
        cols = slice(kt * S5_KT, (kt + 1) * S5_KT)
        bu = jnp.dot(xb[:, cols], wb_ref[kt], preferred_element_type=F32)
        for j in range(nb):
            sr_ref[j] = bu[:, j * LANES:(j + 1) * LANES]
            si_ref[j] = bu[:, S5_SL + j * LANES:S5_SL + (j + 1) * LANES]

        def lane_block(j, carry, kt=kt):
            re_row = 2 * kt * nb + j
            im_row = re_row + nb
            cr, ci = _s5_scan_block(
                sr_ref.at[j], si_ref.at[j],
                lam_ref[pl.ds(re_row, 1), :], lam_ref[pl.ds(im_row, 1), :],
                carry_ref[pl.ds(re_row, 1), :], carry_ref[pl.ds(im_row, 1), :],
                pr_ref, pi_ref, r)
            carry_ref[pl.ds(re_row, 1), :] = cr
            carry_ref[pl.ds(im_row, 1), :] = ci
            return carry

        lax.fori_loop(0, nb, lane_block, 0)
        hcat = jnp.concatenate([sr_ref[j].astype(BF16) for j in range(nb)]
                               + [si_ref[j].astype(BF16) for j in range(nb)], axis=1)
        y_ref[:, cols] = (jnp.dot(hcat, wc_ref[kt], preferred_element_type=F32)
                          + d_ref[:, cols] * x[:, cols])
    ge = jax.nn.gelu(y_ref[...]).astype(BF16)
    vg = jnp.dot(ge, wout_ref[...], preferred_element_type=F32)
    m = vg[:, :D_MODEL] * jax.nn.sigmoid(vg[:, D_MODEL:])
    o_ref[0] = _ln(ALPHA * x + m, g_ref[...], b_ref[...])


def _s5_tables(a_re, a_im, log_dt, b_re, b_im, c_re, c_im, d_skip):
    lr = a_re.astype(F32)
    li = a_im.astype(F32)
    dt = jnp.exp(log_dt.astype(F32))[:, None]
    mag = jnp.exp(lr * dt)
    ang = li * dt
    lb_re = mag * jnp.cos(ang)
    lb_im = mag * jnp.sin(ang)
    den = lr * lr + li * li
    nr = lb_re - 1.0
    ni = lb_im
    z_re = (nr * lr + ni * li) / den
    z_im = (ni * lr - nr * li) / den
    br = b_re.astype(F32)
    bi = b_im.astype(F32)
    bb_re = z_re[..., None] * br - z_im[..., None] * bi
    bb_im = z_re[..., None] * bi + z_im[..., None] * br
    gk = S5_KT // S5_GROUP
    eye = jnp.eye(gk, dtype=F32)

    def blockdiag_b(m):
        m = m.reshape(S5_NKT, gk, S5_STATE, S5_GROUP)
        w = jnp.einsum('kgph,gj->kghjp', m, eye)
        return w.reshape(S5_NKT, S5_KT, S5_SL)

    def blockdiag_c(m):
        m = m.reshape(S5_NKT, gk, S5_GROUP, S5_STATE)
        w = jnp.einsum('kghp,gj->kgpjh', m, eye)
        return w.reshape(S5_NKT, S5_SL, S5_KT)

    wb = jnp.concatenate([blockdiag_b(bb_re), blockdiag_b(bb_im)], axis=2).astype(BF16)
    wc = jnp.concatenate([blockdiag_c(c_re.astype(F32)), -blockdiag_c(c_im.astype(F32))],
                         axis=1).astype(BF16)
    lam = jnp.stack([lb_re.reshape(S5_NKT, S5_SL), lb_im.reshape(S5_NKT, S5_SL)], axis=1)
    lam = lam.reshape(2 * S5_NKT * (S5_SL // LANES), LANES)
    return wb, wc, lam, d_skip.astype(F32).reshape(1, D_MODEL)


def _s5_ln(x3, wb, wc, lam, dsk, wout, g, b):
    bsz, seq, d = x3.shape
    t = min(TOK_T, seq)
    r = t // CH
    tok = pl.BlockSpec((1, t, d), lambda bi, i: (bi, i, 0))
    return pl.pallas_call(
        _s5_ln_kernel,
        grid=(bsz, seq // t),
        in_specs=[tok, _const_spec(wb.shape), _const_spec(wc.shape), _const_spec(lam.shape),
                  _const_spec(dsk.shape), _const_spec(wout.shape), _const_spec(g.shape),
                  _const_spec(b.shape)],
        out_specs=tok,
        out_shape=jax.ShapeDtypeStruct((bsz, seq, d), F32),
        scratch_shapes=[pltpu.VMEM((S5_SL // LANES, t, LANES), F32),
                        pltpu.VMEM((S5_SL // LANES, t, LANES), F32),
                        pltpu.VMEM((r, LANES), F32), pltpu.VMEM((r, LANES), F32),
                        pltpu.VMEM((t, d), F32), pltpu.VMEM(lam.shape, F32)],
        compiler_params=_params(2),
        name="s5_ln",
    )(x3, wb, wc, lam, dsk, wout, g, b)


def kernel(x, ffn1_w_in, ffn1_w_out, ln1_g, ln1_b, lnm_g, lnm_b, ffn2_w_in, ffn2_w_out, ln2_g, ln2_b, fox_w_in, fox_b_f, fox_w_o, s5_a_re, s5_a_im, s5_log_dt, s5_b_re, s5_b_im, s5_c_re, s5_c_im, s5_d, s5_w_out):
    bsz, seq, d = x.shape
    n = bsz * seq
    row = lambda v: v.astype(F32).reshape(1, d)
    x2 = x.reshape(n, d)
    for i in range(DEPTH):
        x2 = _ffn_ln(x2, ffn1_w_in[i].astype(BF16), ffn1_w_out[i].astype(BF16),
                     row(ln1_g[i]), row(ln1_b[i]))
        j = i // 2
        if i % 2 == 0:
            w = fox_w_in[j]
            scale = LOG2E / (HEAD_DIM ** 0.5)
            wqk = jnp.concatenate([w[:, :d] * scale, w[:, d:2 * d]], axis=1).astype(BF16)
            wvt = w[:, 2 * d:3 * d].T.astype(BF16)
            wf = jnp.pad(w[:, 3 * d:], ((0, 0), (0, LANES - FOX_HEADS))).astype(BF16)
            bf = jnp.pad(fox_b_f[j].astype(F32), (0, LANES - FOX_HEADS)).reshape(1, LANES)
            q, k, vt, cq, ck = _fox_proj(x2.reshape(bsz, seq, d), wqk, wvt, wf, bf)
            a = _fox_attn(q, cq, k, ck, vt)
            x2 = _proj_ln(x2, a.reshape(n, d), fox_w_o[j].astype(BF16), row(lnm_g[i]), row(lnm_b[i]))
        else:
            wb, wc, lam, dsk = _s5_tables(s5_a_re[j], s5_a_im[j], s5_log_dt[j], s5_b_re[j],
                                          s5_b_im[j], s5_c_re[j], s5_c_im[j], s5_d[j])
            x2 = _s5_ln(x2.reshape(bsz, seq, d), wb, wc, lam, dsk, s5_w_out[j].astype(BF16),
                        row(lnm_g[i]), row(lnm_b[i])).reshape(n, d)
        x2 = _ffn_ln(x2, ffn2_w_in[i].astype(BF16), ffn2_w_out[i].astype(BF16),
                     row(ln2_g[i]), row(ln2_b[i]))
    return x2.reshape(bsz, seq, d)
```

```python
import jax
import jax.numpy as jnp
from jax import lax
from jax.experimental import pallas as pl
from jax.experimental.pallas import tpu as pltpu

F32 = jnp.float32
BF16 = jnp.bfloat16

D_MODEL = 1024
DEPTH = 4
FOX_HEADS = 16
HEAD_DIM = D_MODEL // FOX_HEADS
S5_GROUP = 16
S5_GROUPS = D_MODEL // S5_GROUP
S5_STATE = 64
D_FF = 2816
ALPHA = (2.0 * DEPTH) ** 0.25
LN_EPS = 1e-5
NEG_INF = -1e30

LANES = 128
MXU_DIM = 256
VMEM_LIMIT = 56 * 1024 * 1024

FFN_TM = 512
FFN_FC = 256
TOK_T = 512
ATT_T = 512
S5_PITCH = TOK_T + 8
S5_UNROLL = 8
S5_KT = MXU_DIM
S5_NKT = D_MODEL // S5_KT
S5_SL = (S5_KT // S5_GROUP) * S5_STATE
C_PARTS = 3
DEN_ROWS = 16
LOG2E = 1.4426950408889634
HEAD_SLOT_SHIFT = 3
HEAD_SLOT = 1 << HEAD_SLOT_SHIFT


def _params(n_grid):
    return pltpu.CompilerParams(dimension_semantics=("arbitrary",) * n_grid,
                                vmem_limit_bytes=VMEM_LIMIT)


def _const_spec(shape):
    nd = len(shape)
    return pl.BlockSpec(shape, lambda *_: (0,) * nd, pipeline_mode=pl.Buffered(1))


def _ln(y, g, b):
    mu = jnp.mean(y, axis=-1, keepdims=True)
    d = y - mu
    var = jnp.mean(d * d, axis=-1, keepdims=True)
    return d * lax.rsqrt(var + LN_EPS) * g + b


def _split3(c):
    hi = c.astype(BF16)
    r1 = c - hi.astype(F32)
    mid = r1.astype(BF16)
    lo = (r1 - mid.astype(F32)).astype(BF16)
    return hi, mid, lo


def _ffn_ln_kernel(x_ref, win_ref, wout_ref, g_ref, b_ref, o_ref, h_ref):
    x = x_ref[...]
    xb = x.astype(BF16)
    for c in range(D_FF // FFN_FC):
        lo = c * FFN_FC
        gate = jnp.dot(xb, win_ref[:, lo:lo + FFN_FC], preferred_element_type=F32)
        up = jnp.dot(xb, win_ref[:, D_FF + lo:D_FF + lo + FFN_FC], preferred_element_type=F32)
        h_ref[:, lo:lo + FFN_FC] = (gate * jax.nn.sigmoid(gate) * up).astype(BF16)
    acc = jnp.dot(h_ref[...], wout_ref[...], preferred_element_type=F32)
    o_ref[...] = _ln(ALPHA * x + 0.5 * acc, g_ref[...], b_ref[...])


def _ffn_ln(x2, win, wout, g, b):
    n, d = x2.shape
    tm = min(FFN_TM, n)
    return pl.pallas_call(
        _ffn_ln_kernel,
        grid=(n // tm,),
        in_specs=[pl.BlockSpec((tm, d), lambda i: (i, 0)),
                  _const_spec(win.shape), _const_spec(wout.shape),
                  _const_spec(g.shape), _const_spec(b.shape)],
        out_specs=pl.BlockSpec((tm, d), lambda i: (i, 0)),
        out_shape=jax.ShapeDtypeStruct((n, d), F32),
        scratch_shapes=[pltpu.VMEM((tm, D_FF), BF16)],
        compiler_params=_params(1),
        name="ffn_ln",
    )(x2, win, wout, g, b)


def _fox_proj_kernel(x_ref, wqk_ref, wvt_ref, wf_ref, bf_ref, eq_ref, ek_ref, oq_ref, ok_ref,
                     q_ref, k_ref, vt_ref, cq_ref, ck_ref, carry_ref):
    @pl.when(pl.program_id(1) == 0)
    def _():
        carry_ref[...] = jnp.zeros_like(carry_ref)

    xb = x_ref[0].astype(BF16)
    t = xb.shape[0]
    d = D_MODEL
    q_ref[0] = jnp.dot(xb, wqk_ref[:, 0:d], preferred_element_type=F32).astype(BF16)
    k_ref[0] = jnp.dot(xb, wqk_ref[:, d:2 * d], preferred_element_type=F32).astype(BF16)
    vt_ref[0, 0] = lax.dot_general(wvt_ref[...], xb, (((1,), (1,)), ((), ())),
                                   preferred_element_type=F32).astype(BF16)

    f = jnp.dot(xb, wf_ref[...], preferred_element_type=F32) + bf_ref[...]
    logf = jnp.minimum(f, 0.0) - jnp.log1p(jnp.exp(-jnp.abs(f)))
    row = lax.broadcasted_iota(jnp.int32, (t, t), 0)
    col = lax.broadcasted_iota(jnp.int32, (t, t), 1)
    tri = (col <= row).astype(BF16)
    pieces = jnp.concatenate(_split3(logf), axis=1)
    cum3 = jnp.dot(tri, pieces, preferred_element_type=F32)
    c = (cum3[:, 0:LANES] + cum3[:, LANES:2 * LANES] + cum3[:, 2 * LANES:3 * LANES]
         + carry_ref[...])
    carry_ref[...] = c[t - 1:t, :]
    ccat = jnp.concatenate(_split3(c * LOG2E), axis=1)
    cq_ref[0] = (jnp.dot(ccat, eq_ref[...], preferred_element_type=F32) + oq_ref[...]).astype(BF16)
    ck_ref[0] = (jnp.dot(ccat, ek_ref[...], preferred_element_type=F32) + ok_ref[...]).astype(BF16)


def _bias_routing():
    eq = jnp.zeros((C_PARTS * LANES, LANES), F32)
    ek = jnp.zeros((C_PARTS * LANES, LANES), F32)
    oq = jnp.zeros((1, LANES), F32)
    ok = jnp.zeros((1, LANES), F32)
    h = jnp.arange(FOX_HEADS)
    for p in range(C_PARTS):
        eq = eq.at[p * LANES + h, h * HEAD_SLOT + p].set(1.0)
        ek = ek.at[p * LANES + h, h * HEAD_SLOT + C_PARTS + p].set(-1.0)
        oq = oq.at[0, h * HEAD_SLOT + C_PARTS + p].set(1.0)
        ok = ok.at[0, h * HEAD_SLOT + p].set(1.0)
    return eq.astype(BF16), ek.astype(BF16), oq, ok


def _fox_proj(x3, wqk, wvt, wf, bf):
    bsz, seq, d = x3.shape
    t = min(ATT_T, seq)
    eq, ek, oq, ok = _bias_routing()
    tok = lambda w: pl.BlockSpec((1, t, w), lambda b, i: (b, i, 0))
    big = jax.ShapeDtypeStruct((bsz, seq, d), BF16)
    small = jax.ShapeDtypeStruct((bsz, seq, LANES), BF16)
    return pl.pallas_call(
        _fox_proj_kernel,
        grid=(bsz, seq // t),
        in_specs=[tok(d), _const_spec(wqk.shape), _const_spec(wvt.shape), _const_spec(wf.shape),
                  _const_spec(bf.shape), _const_spec(eq.shape), _const_spec(ek.shape),
                  _const_spec(oq.shape), _const_spec(ok.shape)],
        out_specs=[tok(d), tok(d), pl.BlockSpec((1, 1, d, t), lambda b, i: (b, i, 0, 0)),
                   tok(LANES), tok(LANES)],
        out_shape=[big, big, jax.ShapeDtypeStruct((bsz, seq // t, d, t), BF16), small, small],
        scratch_shapes=[pltpu.VMEM((1, LANES), F32)],
        compiler_params=_params(2),
        name="fox_proj",
    )(x3, wqk, wvt, wf, bf, eq, ek, oq, ok)


def _fox_attn_kernel(q_ref, cq_ref, k_ref, ck_ref, vt_ref, o_ref,
                     m_ref, acc_ref, sa_ref, sb_ref):
    pair = pl.program_id(1)
    qi = pl.program_id(2)
    qp = q_ref[0]
    cq = cq_ref[0]
    t = qp.shape[0]
    lane = lax.broadcasted_iota(jnp.int32, (t, LANES), 1)
    key_pos = lax.broadcasted_iota(jnp.int32, (t, t), 0)
    qry_pos = lax.broadcasted_iota(jnp.int32, (t, t), 1)
    first_half = lane < HEAD_DIM
    lhs = []
    for hh in range(2):
        head = 2 * pair + hh
        qm = jnp.where(first_half if hh == 0 else jnp.logical_not(first_half), qp, jnp.zeros_like(qp))
        cm = jnp.where((lane >> HEAD_SLOT_SHIFT) == head, cq, jnp.zeros_like(cq))
        lhs.append(jnp.concatenate([qm, cm], axis=1))
    m_ref[...] = jnp.full(m_ref.shape, NEG_INF, F32)
    acc_ref[...] = jnp.zeros(acc_ref.shape, F32)
    ones_rows = jnp.ones((DEN_ROWS, t), BF16)

    def scores(j, s_ref):
        off = pl.multiple_of(j * t, t)
        rhs = jnp.concatenate([k_ref[0, pl.ds(off, t), :], ck_ref[0, pl.ds(off, t), :]], axis=1)
        for hh in range(2):
            s_ref[hh] = lax.dot_general(rhs, lhs[hh], (((1,), (1,)), ((), ())),
                                        preferred_element_type=F32)

    def absorb(j, s_ref, masked):
        for hh in range(2):
            st = s_ref[hh]
            if masked:
                st = jnp.where(key_pos <= qry_pos, st, NEG_INF)
            m_prev = m_ref[hh]
            m_new = jnp.maximum(m_prev, jnp.max(st, axis=0, keepdims=True))
            pt = jnp.exp2(st - m_new)
            a = jnp.exp2(m_prev - m_new)
            vt = jnp.concatenate([vt_ref[0, j, hh * HEAD_DIM:(hh + 1) * HEAD_DIM, :], ones_rows],
                                 axis=0)
            acc_ref[hh] = a * acc_ref[hh] + jnp.dot(vt, pt.astype(BF16),
                                                    preferred_element_type=F32)
            m_ref[hh] = m_new

    scores(0, sa_ref)

    def body(i, carry):
        j = 2 * i
        scores(j + 1, sb_ref)
        absorb(j, sa_ref, False)
        scores(j + 2, sa_ref)
        absorb(j + 1, sb_ref, False)
        return carry

    lax.fori_loop(0, qi // 2, body, 0)

    @pl.when(qi % 2 == 0)
    def _():
        absorb(qi, sa_ref, True)

    @pl.when(qi % 2 == 1)
    def _():
        scores(qi, sb_ref)
        absorb(qi - 1, sa_ref, False)
        absorb(qi, sb_ref, True)

    ot = jnp.concatenate([acc_ref[hh, :HEAD_DIM, :] / acc_ref[hh, HEAD_DIM:HEAD_DIM + 1, :]
                          for hh in range(2)], axis=0)
    o_ref[0] = ot.T.astype(BF16)


def _fox_attn(q, cq, k, ck, vt):
    bsz, seq, d = q.shape
    t = min(ATT_T, seq)
    return pl.pallas_call(
        _fox_attn_kernel,
        grid=(bsz, d // LANES, seq // t),
        in_specs=[pl.BlockSpec((1, t, LANES), lambda b, p, i: (b, i, p)),
                  pl.BlockSpec((1, t, LANES), lambda b, p, i: (b, i, 0)),
                  pl.BlockSpec((1, seq, LANES), lambda b, p, i: (b, 0, p)),
                  pl.BlockSpec((1, seq, LANES), lambda b, p, i: (b, 0, 0)),
                  pl.BlockSpec((1, seq // t, LANES, t), lambda b, p, i: (b, 0, p, 0))],
        out_specs=pl.BlockSpec((1, t, LANES), lambda b, p, i: (b, i, p)),
        out_shape=jax.ShapeDtypeStruct((bsz, seq, d), BF16),
        scratch_shapes=[pltpu.VMEM((2, 1, t), F32),
                        pltpu.VMEM((2, HEAD_DIM + DEN_ROWS, t), F32),
                        pltpu.VMEM((2, t, t), F32), pltpu.VMEM((2, t, t), F32)],
        compiler_params=_params(3),
        name="fox_attn",
    )(q, cq, k, ck, vt)


def _proj_ln_kernel(x_ref, a_ref, w_ref, g_ref, b_ref, o_ref):
    m = jnp.dot(a_ref[...], w_ref[...], preferred_element_type=F32)
    o_ref[...] = _ln(ALPHA * x_ref[...] + m, g_ref[...], b_ref[...])


def _proj_ln(x2, a2, w, g, b):
    n, d = x2.shape
    tm = min(TOK_T, n)
    tile = pl.BlockSpec((tm, d), lambda i: (i, 0))
    return pl.pallas_call(
        _proj_ln_kernel,
        grid=(n // tm,),
        in_specs=[tile, tile, _const_spec(w.shape), _const_spec(g.shape), _const_spec(b.shape)],
        out_specs=tile,
        out_shape=jax.ShapeDtypeStruct((n, d), F32),
        compiler_params=_params(1),
        name="proj_ln",
    )(x2, a2, w, g, b)


def _cmul(ar, ai, br, bi):
    return ar * br - ai * bi, ar * bi + ai * br


def _s5_ln_kernel(x_ref, wb_ref, wc_ref, lam_ref, d_ref, wout_ref, g_ref, b_ref, o_ref,
                  s_ref, y_ref, carry_ref):
    @pl.when(pl.program_id(1) == 0)
    def _():
        carry_ref[...] = jnp.zeros_like(carry_ref)

    x = x_ref[0]
    xb = x.astype(BF16)
    t = x.shape[0]
    nb = S5_SL // LANES
    slab = lambda j: slice(j * S5_PITCH, j * S5_PITCH + t)
    for kt in range(S5_NKT):
        cols = slice(kt * S5_KT, (kt + 1) * S5_KT)
        bu = jnp.dot(xb[:, cols], wb_ref[kt], preferred_element_type=F32)
        for j in range(nb):
            s_ref[2 * kt, slab(j), :] = bu[:, j * LANES:(j + 1) * LANES]
            s_ref[2 * kt + 1, slab(j), :] = bu[:, S5_SL + j * LANES:S5_SL + (j + 1) * LANES]

    lam = [lam_ref[i * nb:(i + 1) * nb, :] for i in range(2 * S5_NKT)]

    def time_step(ti, h):
        now = pl.ds(ti, nb, stride=S5_PITCH)
        out = []
        for kt in range(S5_NKT):
            mr, mi = _cmul(lam[2 * kt], lam[2 * kt + 1], h[2 * kt], h[2 * kt + 1])
            hr = mr + s_ref[2 * kt, now, :]
            hi = mi + s_ref[2 * kt + 1, now, :]
            s_ref[2 * kt, now, :] = hr
            s_ref[2 * kt + 1, now, :] = hi
            out += [hr, hi]
        return tuple(out)

    h0 = tuple(carry_ref[i * nb:(i + 1) * nb, :] for i in range(2 * S5_NKT))
    h = lax.fori_loop(0, t, time_step, h0, unroll=S5_UNROLL)
    for i in range(2 * S5_NKT):
        carry_ref[i * nb:(i + 1) * nb, :] = h[i]

    for kt in range(S5_NKT):
        cols = slice(kt * S5_KT, (kt + 1) * S5_KT)
        hcat = jnp.concatenate([s_ref[2 * kt + part, slab(j), :].astype(BF16)
                                for part in range(2) for j in range(nb)], axis=1)
        y_ref[:, cols] = (jnp.dot(hcat, wc_ref[kt], preferred_element_type=F32)
                          + d_ref[:, cols] * x[:, cols])
    ge = jax.nn.gelu(y_ref[...]).astype(BF16)
    vg = jnp.dot(ge, wout_ref[...], preferred_element_type=F32)
    m = vg[:, :D_MODEL] * jax.nn.sigmoid(vg[:, D_MODEL:])
    o_ref[0] = _ln(ALPHA * x + m, g_ref[...], b_ref[...])


def _s5_tables(a_re, a_im, log_dt, b_re, b_im, c_re, c_im, d_skip):
    lr = a_re.astype(F32)
    li = a_im.astype(F32)
    dt = jnp.exp(log_dt.astype(F32))[:, None]
    mag = jnp.exp(lr * dt)
    ang = li * dt
    lb_re = mag * jnp.cos(ang)
    lb_im = mag * jnp.sin(ang)
    den = lr * lr + li * li
    nr = lb_re - 1.0
    ni = lb_im
    z_re = (nr * lr + ni * li) / den
    z_im = (ni * lr - nr * li) / den
    br = b_re.astype(F32)
    bi = b_im.astype(F32)
    bb_re = z_re[..., None] * br - z_im[..., None] * bi
    bb_im = z_re[..., None] * bi + z_im[..., None] * br
    gk = S5_KT // S5_GROUP
    eye = jnp.eye(gk, dtype=F32)

    def blockdiag_b(m):
        m = m.reshape(S5_NKT, gk, S5_STATE, S5_GROUP)
        w = jnp.einsum('kgph,gj->kghjp', m, eye)
        return w.reshape(S5_NKT, S5_KT, S5_SL)

    def blockdiag_c(m):
        m = m.reshape(S5_NKT, gk, S5_GROUP, S5_STATE)
        w = jnp.einsum('kghp,gj->kgpjh', m, eye)
        return w.reshape(S5_NKT, S5_SL, S5_KT)

    wb = jnp.concatenate([blockdiag_b(bb_re), blockdiag_b(bb_im)], axis=2).astype(BF16)
    wc = jnp.concatenate([blockdiag_c(c_re.astype(F32)), -blockdiag_c(c_im.astype(F32))],
                         axis=1).astype(BF16)
    lam = jnp.stack([lb_re.reshape(S5_NKT, S5_SL), lb_im.reshape(S5_NKT, S5_SL)], axis=1)
    lam = lam.reshape(2 * S5_NKT * (S5_SL // LANES), LANES)
    return wb, wc, lam, d_skip.astype(F32).reshape(1, D_MODEL)


def _s5_ln(x3, wb, wc, lam, dsk, wout, g, b):
    bsz, seq, d = x3.shape
    t = min(TOK_T, seq)
    tok = pl.BlockSpec((1, t, d), lambda bi, i: (bi, i, 0))
    return pl.pallas_call(
        _s5_ln_kernel,
        grid=(bsz, seq // t),
        in_specs=[tok, _const_spec(wb.shape), _const_spec(wc.shape), _const_spec(lam.shape),
                  _const_spec(dsk.shape), _const_spec(wout.shape), _const_spec(g.shape),
                  _const_spec(b.shape)],
        out_specs=tok,
        out_shape=jax.ShapeDtypeStruct((bsz, seq, d), F32),
        scratch_shapes=[pltpu.VMEM((2 * S5_NKT, (S5_SL // LANES) * S5_PITCH, LANES), F32),
                        pltpu.VMEM((t, d), F32), pltpu.VMEM(lam.shape, F32)],
        compiler_params=_params(2),
        name="s5_ln",
    )(x3, wb, wc, lam, dsk, wout, g, b)


def kernel(x, ffn1_w_in, ffn1_w_out, ln1_g, ln1_b, lnm_g, lnm_b, ffn2_w_in, ffn2_w_out, ln2_g, ln2_b, fox_w_in, fox_b_f, fox_w_o, s5_a_re, s5_a_im, s5_log_dt, s5_b_re, s5_b_im, s5_c_re, s5_c_im, s5_d, s5_w_out):
    bsz, seq, d = x.shape
    n = bsz * seq
    row = lambda v: v.astype(F32).reshape(1, d)
    x2 = x.reshape(n, d)
    for i in range(DEPTH):
        x2 = _ffn_ln(x2, ffn1_w_in[i].astype(BF16), ffn1_w_out[i].astype(BF16),
                     row(ln1_g[i]), row(ln1_b[i]))
        j = i // 2
        if i % 2 == 0:
            w = fox_w_in[j]
            scale = LOG2E / (HEAD_DIM ** 0.5)
            wqk = jnp.concatenate([w[:, :d] * scale, w[:, d:2 * d]], axis=1).astype(BF16)
            wvt = w[:, 2 * d:3 * d].T.astype(BF16)
            wf = jnp.pad(w[:, 3 * d:], ((0, 0), (0, LANES - FOX_HEADS))).astype(BF16)
            bf = jnp.pad(fox_b_f[j].astype(F32), (0, LANES - FOX_HEADS)).reshape(1, LANES)
            q, k, vt, cq, ck = _fox_proj(x2.reshape(bsz, seq, d), wqk, wvt, wf, bf)
            a = _fox_attn(q, cq, k, ck, vt)
            x2 = _proj_ln(x2, a.reshape(n, d), fox_w_o[j].astype(BF16), row(lnm_g[i]), row(lnm_b[i]))
        else:
            wb, wc, lam, dsk = _s5_tables(s5_a_re[j], s5_a_im[j], s5_log_dt[j], s5_b_re[j],
                                          s5_b_im[j], s5_c_re[j], s5_c_im[j], s5_d[j])
            x2 = _s5_ln(x2.reshape(bsz, seq, d), wb, wc, lam, dsk, s5_w_out[j].astype(BF16),
                        row(lnm_g[i]), row(lnm_b[i])).reshape(n, d)
        x2 = _ffn_ln(x2, ffn2_w_in[i].astype(BF16), ffn2_w_out[i].astype(BF16),
                     row(ln2_g[i]), row(ln2_b[i]))
    return x2.reshape(bsz, seq, d)
```

```python
import functools

import jax
import jax.numpy as jnp
from jax import lax
from jax.experimental import pallas as pl
from jax.experimental.pallas import tpu as pltpu

F32 = jnp.float32
BF16 = jnp.bfloat16

D_MODEL = 1024
DEPTH = 4
FOX_HEADS = 16
HEAD_DIM = D_MODEL // FOX_HEADS
S5_GROUP = 16
S5_GROUPS = D_MODEL // S5_GROUP
S5_STATE = 64
D_FF = 2816
ALPHA = (2.0 * DEPTH) ** 0.25
LN_EPS = 1e-5
NEG_INF = -1e30

LANES = 128
MXU_DIM = 256
VMEM_LIMIT = 56 * 1024 * 1024

FFN_TM = 1024
FFN_PARTS = 2
FFN_FC = 256
TOK_T = 512
ATT_T = 512
S5_PITCH = TOK_T + 8
S5_UNROLL = 8
S5_KT = MXU_DIM
S5_NKT = D_MODEL // S5_KT
S5_SL = (S5_KT // S5_GROUP) * S5_STATE
C_PARTS = 3
DEN_ROWS = 16
LOG2E = 1.4426950408889634
HEAD_SLOT_SHIFT = 3
HEAD_SLOT = 1 << HEAD_SLOT_SHIFT


def _params(n_grid):
    return pltpu.CompilerParams(dimension_semantics=("arbitrary",) * n_grid,
                                vmem_limit_bytes=VMEM_LIMIT)


def _const_spec(shape):
    nd = len(shape)
    return pl.BlockSpec(shape, lambda *_: (0,) * nd, pipeline_mode=pl.Buffered(1))


def _layer_spec(shape, layer):
    nd = len(shape) - 1
    return pl.BlockSpec((None,) + tuple(shape[1:]), lambda *_: (layer,) + (0,) * nd,
                        pipeline_mode=pl.Buffered(1))


def _ln(y, g, b):
    mu = jnp.mean(y, axis=-1, keepdims=True)
    d = y - mu
    var = jnp.mean(d * d, axis=-1, keepdims=True)
    return d * lax.rsqrt(var + LN_EPS) * g + b


def _split3(c):
    hi = c.astype(BF16)
    r1 = c - hi.astype(F32)
    mid = r1.astype(BF16)
    lo = (r1 - mid.astype(F32)).astype(BF16)
    return hi, mid, lo


def _ffn_ln_kernel(*refs, with_proj):
    if with_proj:
        x_ref, a_ref, wo_ref, gm_ref, bm_ref, win_ref, wout_ref, g_ref, b_ref, o_ref, h_ref = refs
    else:
        x_ref, win_ref, wout_ref, g_ref, b_ref, o_ref, h_ref = refs
    part_rows = x_ref.shape[0] // FFN_PARTS
    for part in range(FFN_PARTS):
        rows = slice(part * part_rows, (part + 1) * part_rows)
        x = x_ref[rows, :]
        if with_proj:
            m = jnp.dot(a_ref[rows, :], wo_ref[...], preferred_element_type=F32)
            x = _ln(ALPHA * x + m, gm_ref[...], bm_ref[...])
        xb = x.astype(BF16)
        for c in range(D_FF // FFN_FC):
            lo = c * FFN_FC
            gate = jnp.dot(xb, win_ref[:, lo:lo + FFN_FC], preferred_element_type=F32)
            up = jnp.dot(xb, win_ref[:, D_FF + lo:D_FF + lo + FFN_FC], preferred_element_type=F32)
            h_ref[rows, lo:lo + FFN_FC] = (gate * jax.nn.sigmoid(gate) * up).astype(BF16)
        acc = jnp.dot(h_ref[rows, :], wout_ref[...], preferred_element_type=F32)
        o_ref[rows, :] = _ln(ALPHA * x + 0.5 * acc, g_ref[...], b_ref[...])


def _ffn_ln(x2, win, wout, g, b, layer, attn=None):
    n, d = x2.shape
    tm = min(FFN_TM, n)
    tile = pl.BlockSpec((tm, d), lambda i: (i, 0))
    args, specs = [x2], [tile]
    if attn is not None:
        a2, wo, gm, bm, j = attn
        args += [a2, wo, gm, bm]
        specs += [tile, _layer_spec(wo.shape, j), _layer_spec(gm.shape, layer),
                  _layer_spec(bm.shape, layer)]
    args += [win, wout, g, b]
    specs += [_layer_spec(a.shape, layer) for a in (win, wout, g, b)]
    return pl.pallas_call(
        functools.partial(_ffn_ln_kernel, with_proj=attn is not None),
        grid=(n // tm,),
        in_specs=specs,
        out_specs=tile,
        out_shape=jax.ShapeDtypeStruct((n, d), F32),
        scratch_shapes=[pltpu.VMEM((tm, D_FF), BF16)],
        compiler_params=_params(1),
        name="ffn_ln",
    )(*args)


def _fox_proj_kernel(x_ref, wqk_ref, wvt_ref, wf_ref, bf_ref, eq_ref, ek_ref, oq_ref, ok_ref,
                     q_ref, k_ref, vt_ref, cq_ref, ck_ref, carry_ref):
    @pl.when(pl.program_id(1) == 0)
    def _():
        carry_ref[...] = jnp.zeros_like(carry_ref)

    xb = x_ref[0].astype(BF16)
    t = xb.shape[0]
    d = D_MODEL
    q_ref[0] = jnp.dot(xb, wqk_ref[:, 0:d], preferred_element_type=F32).astype(BF16)
    k_ref[0] = jnp.dot(xb, wqk_ref[:, d:2 * d], preferred_element_type=F32).astype(BF16)
    vt_ref[0, 0] = lax.dot_general(wvt_ref[...], xb, (((1,), (1,)), ((), ())),
                                   preferred_element_type=F32).astype(BF16)

    f = jnp.dot(xb, wf_ref[...], preferred_element_type=F32) + bf_ref[...]
    logf = jnp.minimum(f, 0.0) - jnp.log1p(jnp.exp(-jnp.abs(f)))
    row = lax.broadcasted_iota(jnp.int32, (t, t), 0)
    col = lax.broadcasted_iota(jnp.int32, (t, t), 1)
    tri = (col <= row).astype(BF16)
    pieces = jnp.concatenate(_split3(logf), axis=1)
    cum3 = jnp.dot(tri, pieces, preferred_element_type=F32)
    c = (cum3[:, 0:LANES] + cum3[:, LANES:2 * LANES] + cum3[:, 2 * LANES:3 * LANES]
         + carry_ref[...])
    carry_ref[...] = c[t - 1:t, :]
    ccat = jnp.concatenate(_split3(c * LOG2E), axis=1)
    cq_ref[0] = (jnp.dot(ccat, eq_ref[...], preferred_element_type=F32) + oq_ref[...]).astype(BF16)
    ck_ref[0] = (jnp.dot(ccat, ek_ref[...], preferred_element_type=F32) + ok_ref[...]).astype(BF16)


def _bias_routing():
    eq = jnp.zeros((C_PARTS * LANES, LANES), F32)
    ek = jnp.zeros((C_PARTS * LANES, LANES), F32)
    oq = jnp.zeros((1, LANES), F32)
    ok = jnp.zeros((1, LANES), F32)
    h = jnp.arange(FOX_HEADS)
    for p in range(C_PARTS):
        eq = eq.at[p * LANES + h, h * HEAD_SLOT + p].set(1.0)
        ek = ek.at[p * LANES + h, h * HEAD_SLOT + C_PARTS + p].set(-1.0)
        oq = oq.at[0, h * HEAD_SLOT + C_PARTS + p].set(1.0)
        ok = ok.at[0, h * HEAD_SLOT + p].set(1.0)
    return eq.astype(BF16), ek.astype(BF16), oq, ok


def _fox_proj(x3, wqk, wvt, wf, bf):
    bsz, seq, d = x3.shape
    t = min(ATT_T, seq)
    eq, ek, oq, ok = _bias_routing()
    tok = lambda w: pl.BlockSpec((1, t, w), lambda b, i: (b, i, 0))
    big = jax.ShapeDtypeStruct((bsz, seq, d), BF16)
    small = jax.ShapeDtypeStruct((bsz, seq, LANES), BF16)
    return pl.pallas_call(
        _fox_proj_kernel,
        grid=(bsz, seq // t),
        in_specs=[tok(d), _const_spec(wqk.shape), _const_spec(wvt.shape), _const_spec(wf.shape),
                  _const_spec(bf.shape), _const_spec(eq.shape), _const_spec(ek.shape),
                  _const_spec(oq.shape), _const_spec(ok.shape)],
        out_specs=[tok(d), tok(d), pl.BlockSpec((1, 1, d, t), lambda b, i: (b, i, 0, 0)),
                   tok(LANES), tok(LANES)],
        out_shape=[big, big, jax.ShapeDtypeStruct((bsz, seq // t, d, t), BF16), small, small],
        scratch_shapes=[pltpu.VMEM((1, LANES), F32)],
        compiler_params=_params(2),
        name="fox_proj",
    )(x3, wqk, wvt, wf, bf, eq, ek, oq, ok)


def _fox_attn_kernel(q_ref, cq_ref, k_ref, ck_ref, vt_ref, o_ref,
                     m_ref, acc_ref, sa_ref, sb_ref):
    pair = pl.program_id(1)
    qi = pl.program_id(2)
    qp = q_ref[0]
    cq = cq_ref[0]
    t = qp.shape[0]
    lane = lax.broadcasted_iota(jnp.int32, (t, LANES), 1)
    key_pos = lax.broadcasted_iota(jnp.int32, (t, t), 0)
    qry_pos = lax.broadcasted_iota(jnp.int32, (t, t), 1)
    first_half = lane < HEAD_DIM
    lhs = []
    for hh in range(2):
        head = 2 * pair + hh
        qm = jnp.where(first_half if hh == 0 else jnp.logical_not(first_half), qp, jnp.zeros_like(qp))
        cm = jnp.where((lane >> HEAD_SLOT_SHIFT) == head, cq, jnp.zeros_like(cq))
        lhs.append(jnp.concatenate([qm, cm], axis=1))
    m_ref[...] = jnp.full(m_ref.shape, NEG_INF, F32)
    acc_ref[...] = jnp.zeros(acc_ref.shape, F32)
    ones_rows = jnp.ones((DEN_ROWS, t), BF16)

    def scores(j, s_ref):
        off = pl.multiple_of(j * t, t)
        rhs = jnp.concatenate([k_ref[0, pl.ds(off, t), :], ck_ref[0, pl.ds(off, t), :]], axis=1)
        for hh in range(2):
            s_ref[hh] = lax.dot_general(rhs, lhs[hh], (((1,), (1,)), ((), ())),
                                        preferred_element_type=F32)

    def absorb(j, s_ref, masked):
        for hh in range(2):
            st = s_ref[hh]
            if masked:
                st = jnp.where(key_pos <= qry_pos, st, NEG_INF)
            m_prev = m_ref[hh]
            m_new = jnp.maximum(m_prev, jnp.max(st, axis=0, keepdims=True))
            pt = jnp.exp2(st - m_new)
            a = jnp.exp2(m_prev - m_new)
            vt = jnp.concatenate([vt_ref[0, j, hh * HEAD_DIM:(hh + 1) * HEAD_DIM, :], ones_rows],
                                 axis=0)
            acc_ref[hh] = a * acc_ref[hh] + jnp.dot(vt, pt.astype(BF16),
                                                    preferred_element_type=F32)
            m_ref[hh] = m_new

    scores(0, sa_ref)

    def body(i, carry):
        j = 2 * i
        scores(j + 1, sb_ref)
        absorb(j, sa_ref, False)
        scores(j + 2, sa_ref)
        absorb(j + 1, sb_ref, False)
        return carry

    lax.fori_loop(0, qi // 2, body, 0)

    @pl.when(qi % 2 == 0)
    def _():
        absorb(qi, sa_ref, True)

    @pl.when(qi % 2 == 1)
    def _():
        scores(qi, sb_ref)
        absorb(qi - 1, sa_ref, False)
        absorb(qi, sb_ref, True)

    ot = jnp.concatenate([acc_ref[hh, :HEAD_DIM, :] / acc_ref[hh, HEAD_DIM:HEAD_DIM + 1, :]
                          for hh in range(2)], axis=0)
    o_ref[0] = ot.T.astype(BF16)


def _fox_attn(q, cq, k, ck, vt):
    bsz, seq, d = q.shape
    t = min(ATT_T, seq)
    return pl.pallas_call(
        _fox_attn_kernel,
        grid=(bsz, d // LANES, seq // t),
        in_specs=[pl.BlockSpec((1, t, LANES), lambda b, p, i: (b, i, p)),
                  pl.BlockSpec((1, t, LANES), lambda b, p, i: (b, i, 0)),
                  pl.BlockSpec((1, seq, LANES), lambda b, p, i: (b, 0, p)),
                  pl.BlockSpec((1, seq, LANES), lambda b, p, i: (b, 0, 0)),
                  pl.BlockSpec((1, seq // t, LANES, t), lambda b, p, i: (b, 0, p, 0))],
        out_specs=pl.BlockSpec((1, t, LANES), lambda b, p, i: (b, i, p)),
        out_shape=jax.ShapeDtypeStruct((bsz, seq, d), BF16),
        scratch_shapes=[pltpu.VMEM((2, 1, t), F32),
                        pltpu.VMEM((2, HEAD_DIM + DEN_ROWS, t), F32),
                        pltpu.VMEM((2, t, t), F32), pltpu.VMEM((2, t, t), F32)],
        compiler_params=_params(3),
        name="fox_attn",
    )(q, cq, k, ck, vt)


def _cmul(ar, ai, br, bi):
    return ar * br - ai * bi, ar * bi + ai * br


def _s5_ln_kernel(x_ref, wb_ref, wc_ref, lam_ref, d_ref, wout_ref, g_ref, b_ref, o_ref,
                  s_ref, y_ref, carry_ref):
    @pl.when(pl.program_id(1) == 0)
    def _():
        carry_ref[...] = jnp.zeros_like(carry_ref)

    x = x_ref[0]
    xb = x.astype(BF16)
    t = x.shape[0]
    nb = S5_SL // LANES
    slab = lambda j: slice(j * S5_PITCH, j * S5_PITCH + t)
    for kt in range(S5_NKT):
        cols = slice(kt * S5_KT, (kt + 1) * S5_KT)
        bu = jnp.dot(xb[:, cols], wb_ref[kt], preferred_element_type=F32)
        for j in range(nb):
            s_ref[2 * kt, slab(j), :] = bu[:, j * LANES:(j + 1) * LANES]
            s_ref[2 * kt + 1, slab(j), :] = bu[:, S5_SL + j * LANES:S5_SL + (j + 1) * LANES]

    lam = [lam_ref[i * nb:(i + 1) * nb, :] for i in range(2 * S5_NKT)]

    def time_step(ti, h):
        now = pl.ds(ti, nb, stride=S5_PITCH)
        out = []
        for kt in range(S5_NKT):
            mr, mi = _cmul(lam[2 * kt], lam[2 * kt + 1], h[2 * kt], h[2 * kt + 1])
            hr = mr + s_ref[2 * kt, now, :]
            hi = mi + s_ref[2 * kt + 1, now, :]
            s_ref[2 * kt, now, :] = hr
            s_ref[2 * kt + 1, now, :] = hi
            out += [hr, hi]
        return tuple(out)

    h0 = tuple(carry_ref[i * nb:(i + 1) * nb, :] for i in range(2 * S5_NKT))
    h = lax.fori_loop(0, t, time_step, h0, unroll=S5_UNROLL)
    for i in range(2 * S5_NKT):
        carry_ref[i * nb:(i + 1) * nb, :] = h[i]

    for kt in range(S5_NKT):
        cols = slice(kt * S5_KT, (kt + 1) * S5_KT)
        hcat = jnp.concatenate([s_ref[2 * kt + part, slab(j), :].astype(BF16)
                                for part in range(2) for j in range(nb)], axis=1)
        y_ref[:, cols] = (jnp.dot(hcat, wc_ref[kt], preferred_element_type=F32)
                          + d_ref[:, cols] * x[:, cols])
    ge = jax.nn.gelu(y_ref[...]).astype(BF16)
    vg = jnp.dot(ge, wout_ref[...], preferred_element_type=F32)
    m = vg[:, :D_MODEL] * jax.nn.sigmoid(vg[:, D_MODEL:])
    o_ref[0] = _ln(ALPHA * x + m, g_ref[...], b_ref[...])


def _s5_tables(a_re, a_im, log_dt, b_re, b_im, c_re, c_im, d_skip):
    lr = a_re.astype(F32)
    li = a_im.astype(F32)
    dt = jnp.exp(log_dt.astype(F32))[:, None]
    mag = jnp.exp(lr * dt)
    ang = li * dt
    lb_re = mag * jnp.cos(ang)
    lb_im = mag * jnp.sin(ang)
    den = lr * lr + li * li
    nr = lb_re - 1.0
    ni = lb_im
    z_re = (nr * lr + ni * li) / den
    z_im = (ni * lr - nr * li) / den
    br = b_re.astype(F32)
    bi = b_im.astype(F32)
    bb_re = z_re[..., None] * br - z_im[..., None] * bi
    bb_im = z_re[..., None] * bi + z_im[..., None] * br
    gk = S5_KT // S5_GROUP
    eye = jnp.eye(gk, dtype=F32)

    def blockdiag_b(m):
        m = m.reshape(S5_NKT, gk, S5_STATE, S5_GROUP)
        w = jnp.einsum('kgph,gj->kghjp', m, eye)
        return w.reshape(S5_NKT, S5_KT, S5_SL)

    def blockdiag_c(m):
        m = m.reshape(S5_NKT, gk, S5_GROUP, S5_STATE)
        w = jnp.einsum('kghp,gj->kgpjh', m, eye)
        return w.reshape(S5_NKT, S5_SL, S5_KT)

    wb = jnp.concatenate([blockdiag_b(bb_re), blockdiag_b(bb_im)], axis=2).astype(BF16)
    wc = jnp.concatenate([blockdiag_c(c_re.astype(F32)), -blockdiag_c(c_im.astype(F32))],
                         axis=1).astype(BF16)
    lam = jnp.stack([lb_re.reshape(S5_NKT, S5_SL), lb_im.reshape(S5_NKT, S5_SL)], axis=1)
    lam = lam.reshape(2 * S5_NKT * (S5_SL // LANES), LANES)
    return wb, wc, lam, d_skip.astype(F32).reshape(1, D_MODEL)


def _s5_ln(x3, wb, wc, lam, dsk, wout, j, g, b, layer):
    bsz, seq, d = x3.shape
    t = min(TOK_T, seq)
    tok = pl.BlockSpec((1, t, d), lambda bi, i: (bi, i, 0))
    return pl.pallas_call(
        _s5_ln_kernel,
        grid=(bsz, seq // t),
        in_specs=[tok, _const_spec(wb.shape), _const_spec(wc.shape), _const_spec(lam.shape),
                  _const_spec(dsk.shape), _layer_spec(wout.shape, j), _layer_spec(g.shape, layer),
                  _layer_spec(b.shape, layer)],
        out_specs=tok,
        out_shape=jax.ShapeDtypeStruct((bsz, seq, d), F32),
        scratch_shapes=[pltpu.VMEM((2 * S5_NKT, (S5_SL // LANES) * S5_PITCH, LANES), F32),
                        pltpu.VMEM((t, d), F32), pltpu.VMEM(lam.shape, F32)],
        compiler_params=_params(2),
        name="s5_ln",
    )(x3, wb, wc, lam, dsk, wout, g, b)


def kernel(x, ffn1_w_in, ffn1_w_out, ln1_g, ln1_b, lnm_g, lnm_b, ffn2_w_in, ffn2_w_out, ln2_g, ln2_b, fox_w_in, fox_b_f, fox_w_o, s5_a_re, s5_a_im, s5_log_dt, s5_b_re, s5_b_im, s5_c_re, s5_c_im, s5_d, s5_w_out):
    bsz, seq, d = x.shape
    n = bsz * seq
    rows = lambda v: v.astype(F32).reshape(v.shape[0], 1, d)
    ffn1 = (ffn1_w_in.astype(BF16), ffn1_w_out.astype(BF16), rows(ln1_g), rows(ln1_b))
    ffn2 = (ffn2_w_in.astype(BF16), ffn2_w_out.astype(BF16), rows(ln2_g), rows(ln2_b))
    gm, bm = rows(lnm_g), rows(lnm_b)
    fox_wo = fox_w_o.astype(BF16)
    s5_wout = s5_w_out.astype(BF16)
    x2 = x.reshape(n, d)
    for i in range(DEPTH):
        x2 = _ffn_ln(x2, *ffn1, i)
        j = i // 2
        if i % 2 == 0:
            w = fox_w_in[j]
            scale = LOG2E / (HEAD_DIM ** 0.5)
            wqk = jnp.concatenate([w[:, :d] * scale, w[:, d:2 * d]], axis=1).astype(BF16)
            wvt = w[:, 2 * d:3 * d].T.astype(BF16)
            wf = jnp.pad(w[:, 3 * d:], ((0, 0), (0, LANES - FOX_HEADS))).astype(BF16)
            bf = jnp.pad(fox_b_f[j].astype(F32), (0, LANES - FOX_HEADS)).reshape(1, LANES)
            q, k, vt, cq, ck = _fox_proj(x2.reshape(bsz, seq, d), wqk, wvt, wf, bf)
            a = _fox_attn(q, cq, k, ck, vt)
            x2 = _ffn_ln(x2, *ffn2, i, attn=(a.reshape(n, d), fox_wo, gm, bm, j))
        else:
            wb, wc, lam, dsk = _s5_tables(s5_a_re[j], s5_a_im[j], s5_log_dt[j], s5_b_re[j],
                                          s5_b_im[j], s5_c_re[j], s5_c_im[j], s5_d[j])
            x2 = _s5_ln(x2.reshape(bsz, seq, d), wb, wc, lam, dsk, s5_wout, j, gm, bm, i
                        ).reshape(n, d)
            x2 = _ffn_ln(x2, *ffn2, i)
    return x2.reshape(bsz, seq, d)
```

```python
import functools

import jax
import jax.numpy as jnp
import numpy as np
from jax import lax
from jax.experimental import pallas as pl
from jax.experimental.pallas import tpu as pltpu

F32 = jnp.float32
BF16 = jnp.bfloat16

D_MODEL = 1024
DEPTH = 4
FOX_HEADS = 16
HEAD_DIM = D_MODEL // FOX_HEADS
S5_GROUP = 16
S5_GROUPS = D_MODEL // S5_GROUP
S5_STATE = 64
D_FF = 2816
ALPHA = (2.0 * DEPTH) ** 0.25
LN_EPS = 1e-5
NEG_INF = -1e30

LANES = 128
MXU_DIM = 256
VMEM_LIMIT = 56 * 1024 * 1024

FFN_TM = 1024
FFN_PARTS = 2
FFN_FC = 256
TOK_T = 512
ATT_T = 512
S5_PITCH = TOK_T + 8
S5_UNROLL = 8
S5_KT = MXU_DIM
S5_NKT = D_MODEL // S5_KT
S5_SL = (S5_KT // S5_GROUP) * S5_STATE
C_PARTS = 3
DEN_ROWS = 16
LOG2E = 1.4426950408889634
HEAD_SLOT_SHIFT = 3
HEAD_SLOT = 1 << HEAD_SLOT_SHIFT


def _params(n_grid):
    return pltpu.CompilerParams(dimension_semantics=("arbitrary",) * n_grid,
                                vmem_limit_bytes=VMEM_LIMIT)


def _const_spec(shape):
    nd = len(shape)
    return pl.BlockSpec(shape, lambda *_: (0,) * nd, pipeline_mode=pl.Buffered(1))


def _layer_spec(shape, layer):
    nd = len(shape) - 1
    return pl.BlockSpec((None,) + tuple(shape[1:]), lambda *_: (layer,) + (0,) * nd,
                        pipeline_mode=pl.Buffered(1))


def _ln(y, g, b):
    mu = jnp.mean(y, axis=-1, keepdims=True)
    d = y - mu
    var = jnp.mean(d * d, axis=-1, keepdims=True)
    return d * lax.rsqrt(var + LN_EPS) * g + b


def _split3(c):
    hi = c.astype(BF16)
    r1 = c - hi.astype(F32)
    mid = r1.astype(BF16)
    lo = (r1 - mid.astype(F32)).astype(BF16)
    return hi, mid, lo


def _ffn_ln_kernel(*refs, with_proj):
    if with_proj:
        x_ref, a_ref, wo_ref, gm_ref, bm_ref, win_ref, wout_ref, g_ref, b_ref, o_ref, h_ref = refs
    else:
        x_ref, win_ref, wout_ref, g_ref, b_ref, o_ref, h_ref = refs
    part_rows = x_ref.shape[0] // FFN_PARTS
    for part in range(FFN_PARTS):
        rows = slice(part * part_rows, (part + 1) * part_rows)
        x = x_ref[rows, :]
        if with_proj:
            m = jnp.dot(a_ref[rows, :], wo_ref[...], preferred_element_type=F32)
            x = _ln(ALPHA * x + m, gm_ref[...], bm_ref[...])
        xb = x.astype(BF16)
        for c in range(D_FF // FFN_FC):
            lo = c * FFN_FC
            gate = jnp.dot(xb, win_ref[:, lo:lo + FFN_FC], preferred_element_type=F32)
            up = jnp.dot(xb, win_ref[:, D_FF + lo:D_FF + lo + FFN_FC], preferred_element_type=F32)
            h_ref[rows, lo:lo + FFN_FC] = (gate * jax.nn.sigmoid(gate) * up).astype(BF16)
        acc = jnp.dot(h_ref[rows, :], wout_ref[...], preferred_element_type=F32)
        o_ref[rows, :] = _ln(ALPHA * x + 0.5 * acc, g_ref[...], b_ref[...])


def _ffn_ln(x2, win, wout, g, b, layer, attn=None):
    n, d = x2.shape
    tm = min(FFN_TM, n)
    tile = pl.BlockSpec((tm, d), lambda i: (i, 0))
    args, specs = [x2], [tile]
    if attn is not None:
        a2, wo, gm, bm, j = attn
        args += [a2, wo, gm, bm]
        specs += [tile, _layer_spec(wo.shape, j), _layer_spec(gm.shape, layer),
                  _layer_spec(bm.shape, layer)]
    args += [win, wout, g, b]
    specs += [_layer_spec(a.shape, layer) for a in (win, wout, g, b)]
    return pl.pallas_call(
        functools.partial(_ffn_ln_kernel, with_proj=attn is not None),
        grid=(n // tm,),
        in_specs=specs,
        out_specs=tile,
        out_shape=jax.ShapeDtypeStruct((n, d), F32),
        scratch_shapes=[pltpu.VMEM((tm, D_FF), BF16)],
        compiler_params=_params(1),
        name="ffn_ln",
    )(*args)


def _fox_proj_kernel(x_ref, wqt_ref, wk_ref, wvt_ref, wf_ref, bf_ref, eqt_ref, ek_ref, oq_ref, ok_ref,
                     qt_ref, k_ref, vt_ref, cqt_ref, ck_ref, carry_ref):
    @pl.when(pl.program_id(1) == 0)
    def _():
        carry_ref[...] = jnp.zeros_like(carry_ref)

    xb = x_ref[0].astype(BF16)
    t = xb.shape[0]
    d = D_MODEL
    k_ref[0] = jnp.dot(xb, wk_ref[...], preferred_element_type=F32).astype(BF16)
    nt = (((1,), (1,)), ((), ()))
    qt_ref[0, 0] = lax.dot_general(wqt_ref[...], xb, nt, preferred_element_type=F32).astype(BF16)
    vt_ref[0, 0] = lax.dot_general(wvt_ref[...], xb, nt, preferred_element_type=F32).astype(BF16)

    f = jnp.dot(xb, wf_ref[...], preferred_element_type=F32) + bf_ref[...]
    logf = jnp.minimum(f, 0.0) - jnp.log1p(jnp.exp(-jnp.abs(f)))
    row = lax.broadcasted_iota(jnp.int32, (t, t), 0)
    col = lax.broadcasted_iota(jnp.int32, (t, t), 1)
    tri = (col <= row).astype(BF16)
    pieces = jnp.concatenate(_split3(logf), axis=1)
    cum3 = jnp.dot(tri, pieces, preferred_element_type=F32)
    c = (cum3[:, 0:LANES] + cum3[:, LANES:2 * LANES] + cum3[:, 2 * LANES:3 * LANES]
         + carry_ref[...])
    carry_ref[...] = c[t - 1:t, :]
    ccat = jnp.concatenate(_split3(c * LOG2E), axis=1)
    cqt_ref[0, 0] = (lax.dot_general(eqt_ref[...], ccat, nt, preferred_element_type=F32)
                     + oq_ref[...]).astype(BF16)
    ck_ref[0] = (jnp.dot(ccat, ek_ref[...], preferred_element_type=F32) + ok_ref[...]).astype(BF16)


def _bias_routing():
    eq = np.zeros((C_PARTS * LANES, LANES), np.float32)
    ek = np.zeros((C_PARTS * LANES, LANES), np.float32)
    oq = np.zeros((1, LANES), np.float32)
    ok = np.zeros((1, LANES), np.float32)
    h = np.arange(FOX_HEADS)
    for p in range(C_PARTS):
        eq[p * LANES + h, h * HEAD_SLOT + p] = 1.0
        ek[p * LANES + h, h * HEAD_SLOT + C_PARTS + p] = -1.0
        oq[0, h * HEAD_SLOT + C_PARTS + p] = 1.0
        ok[0, h * HEAD_SLOT + p] = 1.0
    return (jnp.asarray(eq.T, BF16), jnp.asarray(ek, BF16), jnp.asarray(oq.T), jnp.asarray(ok))


def _fox_proj(x3, wqt, wk, wvt, wf, bf):
    bsz, seq, d = x3.shape
    t = min(ATT_T, seq)
    eqt, ek, oq, ok = _bias_routing()
    tok = lambda w: pl.BlockSpec((1, t, w), lambda b, i: (b, i, 0))
    tr = lambda w: pl.BlockSpec((1, 1, w, t), lambda b, i: (b, i, 0, 0))
    big = jax.ShapeDtypeStruct((bsz, seq, d), BF16)
    big_t = jax.ShapeDtypeStruct((bsz, seq // t, d, t), BF16)
    small = jax.ShapeDtypeStruct((bsz, seq, LANES), BF16)
    return pl.pallas_call(
        _fox_proj_kernel,
        grid=(bsz, seq // t),
        in_specs=[tok(d), _const_spec(wqt.shape), _const_spec(wk.shape), _const_spec(wvt.shape),
                  _const_spec(wf.shape), _const_spec(bf.shape), _const_spec(eqt.shape),
                  _const_spec(ek.shape), _const_spec(oq.shape), _const_spec(ok.shape)],
        out_specs=[tr(d), tok(d), tr(d), tr(LANES), tok(LANES)],
        out_shape=[big_t, big, big_t, jax.ShapeDtypeStruct((bsz, seq // t, LANES, t), BF16), small],
        scratch_shapes=[pltpu.VMEM((1, LANES), F32)],
        compiler_params=_params(2),
        name="fox_proj",
    )(x3, wqt, wk, wvt, wf, bf, eqt, ek, oq, ok)


def _fox_attn_kernel(qt_ref, cqt_ref, k_ref, ck_ref, vt_ref, o_ref,
                     m_ref, acc_ref, sa_ref, sb_ref):
    pair = pl.program_id(1)
    n_q, _, t = qt_ref.shape[1:]
    feat = lax.broadcasted_iota(jnp.int32, (LANES, t), 0)
    key_pos = lax.broadcasted_iota(jnp.int32, (t, t), 0)
    qry_pos = lax.broadcasted_iota(jnp.int32, (t, t), 1)
    first_half = feat < HEAD_DIM
    ones_rows = jnp.ones((DEN_ROWS, t), BF16)

    def scores(j, lhs_t, s_ref):
        off = pl.multiple_of(j * t, t)
        rhs = jnp.concatenate([k_ref[0, pl.ds(off, t), :], ck_ref[0, pl.ds(off, t), :]], axis=1)
        for hh in range(2):
            s_ref[hh] = jnp.dot(rhs, lhs_t[hh], preferred_element_type=F32)

    def absorb(j, s_ref, masked):
        for hh in range(2):
            st = s_ref[hh]
            if masked:
                st = jnp.where(key_pos <= qry_pos, st, NEG_INF)
            m_prev = m_ref[hh]
            m_new = jnp.maximum(m_prev, jnp.max(st, axis=0, keepdims=True))
            pt = jnp.exp2(st - m_new)
            a = jnp.exp2(m_prev - m_new)
            vt = jnp.concatenate([vt_ref[0, j, hh * HEAD_DIM:(hh + 1) * HEAD_DIM, :], ones_rows],
                                 axis=0)
            acc_ref[hh] = a * acc_ref[hh] + jnp.dot(vt, pt.astype(BF16),
                                                    preferred_element_type=F32)
            m_ref[hh] = m_new

    for qi in range(n_q):
        qt = qt_ref[0, qi]
        cqt = cqt_ref[0, qi]
        lhs_t = []
        for hh in range(2):
            head = 2 * pair + hh
            qm = jnp.where(first_half if hh == 0 else jnp.logical_not(first_half), qt,
                           jnp.zeros_like(qt))
            cm = jnp.where((feat >> HEAD_SLOT_SHIFT) == head, cqt, jnp.zeros_like(cqt))
            lhs_t.append(jnp.concatenate([qm, cm], axis=0))
        m_ref[...] = jnp.full(m_ref.shape, NEG_INF, F32)
        acc_ref[...] = jnp.zeros(acc_ref.shape, F32)

        scores(0, lhs_t, sa_ref)

        def body(i, carry, lhs_t=lhs_t):
            j = 2 * i
            scores(j + 1, lhs_t, sb_ref)
            absorb(j, sa_ref, False)
            scores(j + 2, lhs_t, sa_ref)
            absorb(j + 1, sb_ref, False)
            return carry

        if qi // 2 > 0:
            lax.fori_loop(0, qi // 2, body, 0)
        if qi % 2 == 0:
            absorb(qi, sa_ref, True)
        else:
            scores(qi, lhs_t, sb_ref)
            absorb(qi - 1, sa_ref, False)
            absorb(qi, sb_ref, True)

        ot = jnp.concatenate([acc_ref[hh, :HEAD_DIM, :] / acc_ref[hh, HEAD_DIM:HEAD_DIM + 1, :]
                              for hh in range(2)], axis=0)
        o_ref[0, qi * t:(qi + 1) * t, :] = ot.T.astype(BF16)


def _fox_attn(qt, cqt, k, ck, vt):
    bsz, seq, d = k.shape
    t = min(ATT_T, seq)
    n_q = seq // t
    return pl.pallas_call(
        _fox_attn_kernel,
        grid=(bsz, d // LANES),
        in_specs=[pl.BlockSpec((1, n_q, LANES, t), lambda b, p: (b, 0, p, 0)),
                  pl.BlockSpec((1, n_q, LANES, t), lambda b, p: (b, 0, 0, 0)),
                  pl.BlockSpec((1, seq, LANES), lambda b, p: (b, 0, p)),
                  pl.BlockSpec((1, seq, LANES), lambda b, p: (b, 0, 0)),
                  pl.BlockSpec((1, n_q, LANES, t), lambda b, p: (b, 0, p, 0))],
        out_specs=pl.BlockSpec((1, seq, LANES), lambda b, p: (b, 0, p)),
        out_shape=jax.ShapeDtypeStruct((bsz, seq, d), BF16),
        scratch_shapes=[pltpu.VMEM((2, 1, t), F32),
                        pltpu.VMEM((2, HEAD_DIM + DEN_ROWS, t), F32),
                        pltpu.VMEM((2, t, t), F32), pltpu.VMEM((2, t, t), F32)],
        compiler_params=_params(2),
        name="fox_attn",
    )(qt, cqt, k, ck, vt)


def _cmul(ar, ai, br, bi):
    return ar * br - ai * bi, ar * bi + ai * br


def _s5_ln_kernel(x_ref, wb_ref, wc_ref, lam_ref, d_ref, wout_ref, g_ref, b_ref, o_ref,
                  s_ref, y_ref, carry_ref):
    @pl.when(pl.program_id(1) == 0)
    def _():
        carry_ref[...] = jnp.zeros_like(carry_ref)

    x = x_ref[0]
    xb = x.astype(BF16)
    t = x.shape[0]
    nb = S5_SL // LANES
    slab = lambda j: slice(j * S5_PITCH, j * S5_PITCH + t)
    for kt in range(S5_NKT):
        cols = slice(kt * S5_KT, (kt + 1) * S5_KT)
        bu = jnp.dot(xb[:, cols], wb_ref[kt], preferred_element_type=F32)
        for j in range(nb):
            s_ref[2 * kt, slab(j), :] = bu[:, j * LANES:(j + 1) * LANES]
            s_ref[2 * kt + 1, slab(j), :] = bu[:, S5_SL + j * LANES:S5_SL + (j + 1) * LANES]

    lam = [lam_ref[i * nb:(i + 1) * nb, :] for i in range(2 * S5_NKT)]

    def time_step(ti, h):
        now = pl.ds(ti, nb, stride=S5_PITCH)
        out = []
        for kt in range(S5_NKT):
            mr, mi = _cmul(lam[2 * kt], lam[2 * kt + 1], h[2 * kt], h[2 * kt + 1])
            hr = mr + s_ref[2 * kt, now, :]
            hi = mi + s_ref[2 * kt + 1, now, :]
            s_ref[2 * kt, now, :] = hr
            s_ref[2 * kt + 1, now, :] = hi
            out += [hr, hi]
        return tuple(out)

    h0 = tuple(carry_ref[i * nb:(i + 1) * nb, :] for i in range(2 * S5_NKT))
    h = lax.fori_loop(0, t, time_step, h0, unroll=S5_UNROLL)
    for i in range(2 * S5_NKT):
        carry_ref[i * nb:(i + 1) * nb, :] = h[i]

    for kt in range(S5_NKT):
        cols = slice(kt * S5_KT, (kt + 1) * S5_KT)
        hcat = jnp.concatenate([s_ref[2 * kt + part, slab(j), :].astype(BF16)
                                for part in range(2) for j in range(nb)], axis=1)
        y_ref[:, cols] = (jnp.dot(hcat, wc_ref[kt], preferred_element_type=F32)
                          + d_ref[:, cols] * x[:, cols])
    ge = jax.nn.gelu(y_ref[...]).astype(BF16)
    vg = jnp.dot(ge, wout_ref[...], preferred_element_type=F32)
    m = vg[:, :D_MODEL] * jax.nn.sigmoid(vg[:, D_MODEL:])
    o_ref[0] = _ln(ALPHA * x + m, g_ref[...], b_ref[...])


def _s5_tables(a_re, a_im, log_dt, b_re, b_im, c_re, c_im, d_skip):
    lr = a_re.astype(F32)
    li = a_im.astype(F32)
    dt = jnp.exp(log_dt.astype(F32))[:, None]
    mag = jnp.exp(lr * dt)
    ang = li * dt
    lb_re = mag * jnp.cos(ang)
    lb_im = mag * jnp.sin(ang)
    den = lr * lr + li * li
    nr = lb_re - 1.0
    ni = lb_im
    z_re = (nr * lr + ni * li) / den
    z_im = (ni * lr - nr * li) / den
    br = b_re.astype(F32)
    bi = b_im.astype(F32)
    bb_re = z_re[..., None] * br - z_im[..., None] * bi
    bb_im = z_re[..., None] * bi + z_im[..., None] * br
    gk = S5_KT // S5_GROUP

    def blockdiag(m, width):
        a, b = m.shape[1:]
        tiled = jnp.tile(m.reshape(S5_NKT, gk, a, b), (1, 1, 1, gk))
        on_diag = (np.arange(gk * b) // b)[None, :] == np.arange(gk)[:, None]
        return jnp.where(on_diag[None, :, None, :], tiled, 0.0).reshape(S5_NKT, gk * a, width)

    blockdiag_b = lambda m: blockdiag(jnp.swapaxes(m, 1, 2), S5_SL)
    blockdiag_c = lambda m: blockdiag(jnp.swapaxes(m, 1, 2), S5_KT)

    wb = jnp.concatenate([blockdiag_b(bb_re), blockdiag_b(bb_im)], axis=2).astype(BF16)
    wc = jnp.concatenate([blockdiag_c(c_re.astype(F32)), -blockdiag_c(c_im.astype(F32))],
                         axis=1).astype(BF16)
    lam = jnp.stack([lb_re.reshape(S5_NKT, S5_SL), lb_im.reshape(S5_NKT, S5_SL)], axis=1)
    lam = lam.reshape(2 * S5_NKT * (S5_SL // LANES), LANES)
    return wb, wc, lam, d_skip.astype(F32).reshape(1, D_MODEL)


def _s5_ln(x3, wb, wc, lam, dsk, wout, j, g, b, layer):
    bsz, seq, d = x3.shape
    t = min(TOK_T, seq)
    tok = pl.BlockSpec((1, t, d), lambda bi, i: (bi, i, 0))
    return pl.pallas_call(
        _s5_ln_kernel,
        grid=(bsz, seq // t),
        in_specs=[tok, _const_spec(wb.shape), _const_spec(wc.shape), _const_spec(lam.shape),
                  _const_spec(dsk.shape), _layer_spec(wout.shape, j), _layer_spec(g.shape, layer),
                  _layer_spec(b.shape, layer)],
        out_specs=tok,
        out_shape=jax.ShapeDtypeStruct((bsz, seq, d), F32),
        scratch_shapes=[pltpu.VMEM((2 * S5_NKT, (S5_SL // LANES) * S5_PITCH, LANES), F32),
                        pltpu.VMEM((t, d), F32), pltpu.VMEM(lam.shape, F32)],
        compiler_params=_params(2),
        name="s5_ln",
    )(x3, wb, wc, lam, dsk, wout, g, b)


def kernel(x, ffn1_w_in, ffn1_w_out, ln1_g, ln1_b, lnm_g, lnm_b, ffn2_w_in, ffn2_w_out, ln2_g, ln2_b, fox_w_in, fox_b_f, fox_w_o, s5_a_re, s5_a_im, s5_log_dt, s5_b_re, s5_b_im, s5_c_re, s5_c_im, s5_d, s5_w_out):
    bsz, seq, d = x.shape
    n = bsz * seq
    rows = lambda v: v.astype(F32).reshape(v.shape[0], 1, d)
    ffn1 = (ffn1_w_in.astype(BF16), ffn1_w_out.astype(BF16), rows(ln1_g), rows(ln1_b))
    ffn2 = (ffn2_w_in.astype(BF16), ffn2_w_out.astype(BF16), rows(ln2_g), rows(ln2_b))
    gm, bm = rows(lnm_g), rows(lnm_b)
    fox_wo = fox_w_o.astype(BF16)
    s5_wout = s5_w_out.astype(BF16)
    x2 = x.reshape(n, d)
    for i in range(DEPTH):
        x2 = _ffn_ln(x2, *ffn1, i)
        j = i // 2
        if i % 2 == 0:
            w = fox_w_in[j]
            scale = LOG2E / (HEAD_DIM ** 0.5)
            wqt = (w[:, :d] * scale).T.astype(BF16)
            wk = w[:, d:2 * d].astype(BF16)
            wvt = w[:, 2 * d:3 * d].T.astype(BF16)
            wf = jnp.pad(w[:, 3 * d:], ((0, 0), (0, LANES - FOX_HEADS))).astype(BF16)
            bf = jnp.pad(fox_b_f[j].astype(F32), (0, LANES - FOX_HEADS)).reshape(1, LANES)
            qt, k, vt, cqt, ck = _fox_proj(x2.reshape(bsz, seq, d), wqt, wk, wvt, wf, bf)
            a = _fox_attn(qt, cqt, k, ck, vt)
            x2 = _ffn_ln(x2, *ffn2, i, attn=(a.reshape(n, d), fox_wo, gm, bm, j))
        else:
            wb, wc, lam, dsk = _s5_tables(s5_a_re[j], s5_a_im[j], s5_log_dt[j], s5_b_re[j],
                                          s5_b_im[j], s5_c_re[j], s5_c_im[j], s5_d[j])
            x2 = _s5_ln(x2.reshape(bsz, seq, d), wb, wc, lam, dsk, s5_wout, j, gm, bm, i
                        ).reshape(n, d)
            x2 = _ffn_ln(x2, *ffn2, i)
    return x2.reshape(bsz, seq, d)
```

```python
import functools

import jax
import jax.numpy as jnp
import numpy as np
from jax import lax
from jax.experimental import pallas as pl
from jax.experimental.pallas import tpu as pltpu

F32 = jnp.float32
BF16 = jnp.bfloat16

D_MODEL = 1024
DEPTH = 4
FOX_HEADS = 16
HEAD_DIM = D_MODEL // FOX_HEADS
S5_GROUP = 16
S5_GROUPS = D_MODEL // S5_GROUP
S5_STATE = 64
D_FF = 2816
ALPHA = (2.0 * DEPTH) ** 0.25
LN_EPS = 1e-5
NEG_INF = -1e30

LANES = 128
MXU_DIM = 256
VMEM_LIMIT = 56 * 1024 * 1024

FFN_TM = 1024
FFN_PARTS = 2
FFN_FC = 256
TOK_T = 512
ATT_T = 512
S5_PITCH = TOK_T + 8
S5_KT = MXU_DIM
S5_NKT = D_MODEL // S5_KT
S5_SL = (S5_KT // S5_GROUP) * S5_STATE
C_PARTS = 3
DEN_ROWS = 16
LOG2E = 1.4426950408889634
HEAD_SLOT_SHIFT = 3
HEAD_SLOT = 1 << HEAD_SLOT_SHIFT


def _params(n_grid):
    return pltpu.CompilerParams(dimension_semantics=("arbitrary",) * n_grid,
                                vmem_limit_bytes=VMEM_LIMIT)


def _const_spec(shape):
    nd = len(shape)
    return pl.BlockSpec(shape, lambda *_: (0,) * nd, pipeline_mode=pl.Buffered(1))


def _layer_spec(shape, layer):
    nd = len(shape) - 1
    return pl.BlockSpec((None,) + tuple(shape[1:]), lambda *_: (layer,) + (0,) * nd,
                        pipeline_mode=pl.Buffered(1))


def _ln(y, g, b):
    mu = jnp.mean(y, axis=-1, keepdims=True)
    d = y - mu
    var = jnp.mean(d * d, axis=-1, keepdims=True)
    return d * lax.rsqrt(var + LN_EPS) * g + b


def _split3(c):
    hi = c.astype(BF16)
    r1 = c - hi.astype(F32)
    mid = r1.astype(BF16)
    lo = (r1 - mid.astype(F32)).astype(BF16)
    return hi, mid, lo


def _ffn_ln_kernel(*refs, with_proj):
    if with_proj:
        x_ref, a_ref, wo_ref, gm_ref, bm_ref, win_ref, wout_ref, g_ref, b_ref, o_ref, h_ref = refs
    else:
        x_ref, win_ref, wout_ref, g_ref, b_ref, o_ref, h_ref = refs
    part_rows = x_ref.shape[0] // FFN_PARTS
    for part in range(FFN_PARTS):
        rows = slice(part * part_rows, (part + 1) * part_rows)
        x = x_ref[rows, :]
        if with_proj:
            m = jnp.dot(a_ref[rows, :], wo_ref[...], preferred_element_type=F32)
            x = _ln(ALPHA * x + m, gm_ref[...], bm_ref[...])
        xb = x.astype(BF16)
        for c in range(D_FF // FFN_FC):
            lo = c * FFN_FC
            gate = jnp.dot(xb, win_ref[:, lo:lo + FFN_FC], preferred_element_type=F32)
            up = jnp.dot(xb, win_ref[:, D_FF + lo:D_FF + lo + FFN_FC], preferred_element_type=F32)
            h_ref[rows, lo:lo + FFN_FC] = (gate * jax.nn.sigmoid(gate) * up).astype(BF16)
        acc = jnp.dot(h_ref[rows, :], wout_ref[...], preferred_element_type=F32)
        o_ref[rows, :] = _ln(ALPHA * x + 0.5 * acc, g_ref[...], b_ref[...])


def _ffn_ln(x2, win, wout, g, b, layer, attn=None):
    n, d = x2.shape
    tm = min(FFN_TM, n)
    tile = pl.BlockSpec((tm, d), lambda i: (i, 0))
    args, specs = [x2], [tile]
    if attn is not None:
        a2, wo, gm, bm, j = attn
        args += [a2, wo, gm, bm]
        specs += [tile, _layer_spec(wo.shape, j), _layer_spec(gm.shape, layer),
                  _layer_spec(bm.shape, layer)]
    args += [win, wout, g, b]
    specs += [_layer_spec(a.shape, layer) for a in (win, wout, g, b)]
    return pl.pallas_call(
        functools.partial(_ffn_ln_kernel, with_proj=attn is not None),
        grid=(n // tm,),
        in_specs=specs,
        out_specs=tile,
        out_shape=jax.ShapeDtypeStruct((n, d), F32),
        scratch_shapes=[pltpu.VMEM((tm, D_FF), BF16)],
        compiler_params=_params(1),
        name="ffn_ln",
    )(*args)


def _fox_proj_kernel(x_ref, wqt_ref, wk_ref, wvt_ref, wf_ref, bf_ref, eqt_ref, ek_ref, oq_ref, ok_ref,
                     qt_ref, k_ref, vt_ref, cqt_ref, ck_ref, carry_ref):
    @pl.when(pl.program_id(1) == 0)
    def _():
        carry_ref[...] = jnp.zeros_like(carry_ref)

    xb = x_ref[0].astype(BF16)
    t = xb.shape[0]
    d = D_MODEL
    k_ref[0] = jnp.dot(xb, wk_ref[...], preferred_element_type=F32).astype(BF16)
    nt = (((1,), (1,)), ((), ()))
    qt_ref[0, 0] = lax.dot_general(wqt_ref[...], xb, nt, preferred_element_type=F32).astype(BF16)
    vt_ref[0, 0] = lax.dot_general(wvt_ref[...], xb, nt, preferred_element_type=F32).astype(BF16)

    f = jnp.dot(xb, wf_ref[...], preferred_element_type=F32) + bf_ref[...]
    logf = jnp.minimum(f, 0.0) - jnp.log1p(jnp.exp(-jnp.abs(f)))
    row = lax.broadcasted_iota(jnp.int32, (t, t), 0)
    col = lax.broadcasted_iota(jnp.int32, (t, t), 1)
    tri = (col <= row).astype(BF16)
    pieces = jnp.concatenate(_split3(logf), axis=1)
    cum3 = jnp.dot(tri, pieces, preferred_element_type=F32)
    c = (cum3[:, 0:LANES] + cum3[:, LANES:2 * LANES] + cum3[:, 2 * LANES:3 * LANES]
         + carry_ref[...])
    carry_ref[...] = c[t - 1:t, :]
    ccat = jnp.concatenate(_split3(c * LOG2E), axis=1)
    cqt_ref[0, 0] = (lax.dot_general(eqt_ref[...], ccat, nt, preferred_element_type=F32)
                     + oq_ref[...]).astype(BF16)
    ck_ref[0] = (jnp.dot(ccat, ek_ref[...], preferred_element_type=F32) + ok_ref[...]).astype(BF16)


def _bias_routing():
    eq = np.zeros((C_PARTS * LANES, LANES), np.float32)
    ek = np.zeros((C_PARTS * LANES, LANES), np.float32)
    oq = np.zeros((1, LANES), np.float32)
    ok = np.zeros((1, LANES), np.float32)
    h = np.arange(FOX_HEADS)
    for p in range(C_PARTS):
        eq[p * LANES + h, h * HEAD_SLOT + p] = 1.0
        ek[p * LANES + h, h * HEAD_SLOT + C_PARTS + p] = -1.0
        oq[0, h * HEAD_SLOT + C_PARTS + p] = 1.0
        ok[0, h * HEAD_SLOT + p] = 1.0
    return (jnp.asarray(eq.T, BF16), jnp.asarray(ek, BF16), jnp.asarray(oq.T), jnp.asarray(ok))


def _fox_proj(x3, wqt, wk, wvt, wf, bf):
    bsz, seq, d = x3.shape
    t = min(ATT_T, seq)
    eqt, ek, oq, ok = _bias_routing()
    tok = lambda w: pl.BlockSpec((1, t, w), lambda b, i: (b, i, 0))
    tr = lambda w: pl.BlockSpec((1, 1, w, t), lambda b, i: (b, i, 0, 0))
    big = jax.ShapeDtypeStruct((bsz, seq, d), BF16)
    big_t = jax.ShapeDtypeStruct((bsz, seq // t, d, t), BF16)
    small = jax.ShapeDtypeStruct((bsz, seq, LANES), BF16)
    return pl.pallas_call(
        _fox_proj_kernel,
        grid=(bsz, seq // t),
        in_specs=[tok(d), _const_spec(wqt.shape), _const_spec(wk.shape), _const_spec(wvt.shape),
                  _const_spec(wf.shape), _const_spec(bf.shape), _const_spec(eqt.shape),
                  _const_spec(ek.shape), _const_spec(oq.shape), _const_spec(ok.shape)],
        out_specs=[tr(d), tok(d), tr(d), tr(LANES), tok(LANES)],
        out_shape=[big_t, big, big_t, jax.ShapeDtypeStruct((bsz, seq // t, LANES, t), BF16), small],
        scratch_shapes=[pltpu.VMEM((1, LANES), F32)],
        compiler_params=_params(2),
        name="fox_proj",
    )(x3, wqt, wk, wvt, wf, bf, eqt, ek, oq, ok)


def _fox_attn_kernel(qt_ref, cqt_ref, k_ref, ck_ref, vt_ref, o_ref,
                     m_ref, acc_ref, sa_ref, sb_ref):
    pair = pl.program_id(1)
    n_q, _, t = qt_ref.shape[1:]
    feat = lax.broadcasted_iota(jnp.int32, (LANES, t), 0)
    key_pos = lax.broadcasted_iota(jnp.int32, (t, t), 0)
    qry_pos = lax.broadcasted_iota(jnp.int32, (t, t), 1)
    first_half = feat < HEAD_DIM
    ones_rows = jnp.ones((DEN_ROWS, t), BF16)

    def scores(j, lhs_t, s_ref):
        off = pl.multiple_of(j * t, t)
        rhs = jnp.concatenate([k_ref[0, pl.ds(off, t), :], ck_ref[0, pl.ds(off, t), :]], axis=1)
        for hh in range(2):
            s_ref[hh] = jnp.dot(rhs, lhs_t[hh], preferred_element_type=F32)

    def absorb(j, s_ref, masked):
        for hh in range(2):
            st = s_ref[hh]
            if masked:
                st = jnp.where(key_pos <= qry_pos, st, NEG_INF)
            m_prev = m_ref[hh]
            m_new = jnp.maximum(m_prev, jnp.max(st, axis=0, keepdims=True))
            pt = jnp.exp2(st - m_new)
            a = jnp.exp2(m_prev - m_new)
            vt = jnp.concatenate([vt_ref[0, j, hh * HEAD_DIM:(hh + 1) * HEAD_DIM, :], ones_rows],
                                 axis=0)
            acc_ref[hh] = a * acc_ref[hh] + jnp.dot(vt, pt.astype(BF16),
                                                    preferred_element_type=F32)
            m_ref[hh] = m_new

    for qi in range(n_q):
        qt = qt_ref[0, qi]
        cqt = cqt_ref[0, qi]
        lhs_t = []
        for hh in range(2):
            head = 2 * pair + hh
            qm = jnp.where(first_half if hh == 0 else jnp.logical_not(first_half), qt,
                           jnp.zeros_like(qt))
            cm = jnp.where((feat >> HEAD_SLOT_SHIFT) == head, cqt, jnp.zeros_like(cqt))
            lhs_t.append(jnp.concatenate([qm, cm], axis=0))
        m_ref[...] = jnp.full(m_ref.shape, NEG_INF, F32)
        acc_ref[...] = jnp.zeros(acc_ref.shape, F32)

        scores(0, lhs_t, sa_ref)

        def body(i, carry, lhs_t=lhs_t):
            j = 2 * i
            scores(j + 1, lhs_t, sb_ref)
            absorb(j, sa_ref, False)
            scores(j + 2, lhs_t, sa_ref)
            absorb(j + 1, sb_ref, False)
            return carry

        if qi // 2 > 0:
            lax.fori_loop(0, qi // 2, body, 0)
        if qi % 2 == 0:
            absorb(qi, sa_ref, True)
        else:
            scores(qi, lhs_t, sb_ref)
            absorb(qi - 1, sa_ref, False)
            absorb(qi, sb_ref, True)

        ot = jnp.concatenate([acc_ref[hh, :HEAD_DIM, :] / acc_ref[hh, HEAD_DIM:HEAD_DIM + 1, :]
                              for hh in range(2)], axis=0)
        o_ref[0, qi * t:(qi + 1) * t, :] = ot.T.astype(BF16)


def _fox_attn(qt, cqt, k, ck, vt):
    bsz, seq, d = k.shape
    t = min(ATT_T, seq)
    n_q = seq // t
    return pl.pallas_call(
        _fox_attn_kernel,
        grid=(bsz, d // LANES),
        in_specs=[pl.BlockSpec((1, n_q, LANES, t), lambda b, p: (b, 0, p, 0)),
                  pl.BlockSpec((1, n_q, LANES, t), lambda b, p: (b, 0, 0, 0)),
                  pl.BlockSpec((1, seq, LANES), lambda b, p: (b, 0, p)),
                  pl.BlockSpec((1, seq, LANES), lambda b, p: (b, 0, 0)),
                  pl.BlockSpec((1, n_q, LANES, t), lambda b, p: (b, 0, p, 0))],
        out_specs=pl.BlockSpec((1, seq, LANES), lambda b, p: (b, 0, p)),
        out_shape=jax.ShapeDtypeStruct((bsz, seq, d), BF16),
        scratch_shapes=[pltpu.VMEM((2, 1, t), F32),
                        pltpu.VMEM((2, HEAD_DIM + DEN_ROWS, t), F32),
                        pltpu.VMEM((2, t, t), F32), pltpu.VMEM((2, t, t), F32)],
        compiler_params=_params(2),
        name="fox_attn",
    )(qt, cqt, k, ck, vt)


def _cmul(ar, ai, br, bi):
    return ar * br - ai * bi, ar * bi + ai * br


def _s5_ln_kernel(x_ref, wb_ref, wc_ref, lam_ref, d_ref, wout_ref, g_ref, b_ref, o_ref,
                  sa_ref, sb_ref, y_ref, carry_ref):
    @pl.when(pl.program_id(1) == 0)
    def _():
        carry_ref[...] = jnp.zeros_like(carry_ref)

    x = x_ref[0]
    xb = x.astype(BF16)
    t = x.shape[0]
    nb = S5_SL // LANES
    per_half = S5_NKT // 2
    slab = lambda j: slice(j * S5_PITCH, j * S5_PITCH + t)
    halves = [(sa_ref, range(0, per_half)), (sb_ref, range(per_half, S5_NKT))]

    def project_in(s_ref, kts):
        for i, kt in enumerate(kts):
            cols = slice(kt * S5_KT, (kt + 1) * S5_KT)
            bu = jnp.dot(xb[:, cols], wb_ref[kt], preferred_element_type=F32)
            for j in range(nb):
                s_ref[2 * i, slab(j), :] = bu[:, j * LANES:(j + 1) * LANES]
                s_ref[2 * i + 1, slab(j), :] = bu[:, S5_SL + j * LANES:S5_SL + (j + 1) * LANES]

    def recur(s_ref, kts):
        rows = lambda kt, part: slice((2 * kt + part) * nb, (2 * kt + part + 1) * nb)
        lam = [(lam_ref[rows(kt, 0), :], lam_ref[rows(kt, 1), :]) for kt in kts]
        h = [(carry_ref[rows(kt, 0), :], carry_ref[rows(kt, 1), :]) for kt in kts]
        for ti in range(t):
            now = pl.ds(ti, nb, stride=S5_PITCH)
            for i in range(len(h)):
                mr, mi = _cmul(lam[i][0], lam[i][1], h[i][0], h[i][1])
                hr = mr + s_ref[2 * i, now, :]
                hi = mi + s_ref[2 * i + 1, now, :]
                s_ref[2 * i, now, :] = hr
                s_ref[2 * i + 1, now, :] = hi
                h[i] = (hr, hi)
        for i, kt in enumerate(kts):
            carry_ref[rows(kt, 0), :] = h[i][0]
            carry_ref[rows(kt, 1), :] = h[i][1]

    def project_out(s_ref, kts):
        for i, kt in enumerate(kts):
            cols = slice(kt * S5_KT, (kt + 1) * S5_KT)
            hcat = jnp.concatenate([s_ref[2 * i + part, slab(j), :].astype(BF16)
                                    for part in range(2) for j in range(nb)], axis=1)
            y_ref[:, cols] = (jnp.dot(hcat, wc_ref[kt], preferred_element_type=F32)
                              + d_ref[:, cols] * x[:, cols])

    (s_a, kts_a), (s_b, kts_b) = halves
    project_in(s_a, kts_a)
    project_in(s_b, kts_b)
    recur(s_a, kts_a)

    @pl.when(pl.program_id(1) >= 0)
    def _():
        recur(s_b, kts_b)
        project_out(s_a, kts_a)

    project_out(s_b, kts_b)
    ge = jax.nn.gelu(y_ref[...]).astype(BF16)
    vg = jnp.dot(ge, wout_ref[...], preferred_element_type=F32)
    m = vg[:, :D_MODEL] * jax.nn.sigmoid(vg[:, D_MODEL:])
    o_ref[0] = _ln(ALPHA * x + m, g_ref[...], b_ref[...])


def _s5_tables(a_re, a_im, log_dt, b_re, b_im, c_re, c_im, d_skip):
    lr = a_re.astype(F32)
    li = a_im.astype(F32)
    dt = jnp.exp(log_dt.astype(F32))[:, None]
    mag = jnp.exp(lr * dt)
    ang = li * dt
    lb_re = mag * jnp.cos(ang)
    lb_im = mag * jnp.sin(ang)
    den = lr * lr + li * li
    nr = lb_re - 1.0
    ni = lb_im
    z_re = (nr * lr + ni * li) / den
    z_im = (ni * lr - nr * li) / den
    br = b_re.astype(F32)
    bi = b_im.astype(F32)
    bb_re = z_re[..., None] * br - z_im[..., None] * bi
    bb_im = z_re[..., None] * bi + z_im[..., None] * br
    gk = S5_KT // S5_GROUP

    def blockdiag(m, width):
        a, b = m.shape[1:]
        tiled = jnp.tile(m.reshape(S5_NKT, gk, a, b), (1, 1, 1, gk))
        on_diag = (np.arange(gk * b) // b)[None, :] == np.arange(gk)[:, None]
        return jnp.where(on_diag[None, :, None, :], tiled, 0.0).reshape(S5_NKT, gk * a, width)

    blockdiag_b = lambda m: blockdiag(jnp.swapaxes(m, 1, 2), S5_SL)
    blockdiag_c = lambda m: blockdiag(jnp.swapaxes(m, 1, 2), S5_KT)

    wb = jnp.concatenate([blockdiag_b(bb_re), blockdiag_b(bb_im)], axis=2).astype(BF16)
    wc = jnp.concatenate([blockdiag_c(c_re.astype(F32)), -blockdiag_c(c_im.astype(F32))],
                         axis=1).astype(BF16)
    lam = jnp.stack([lb_re.reshape(S5_NKT, S5_SL), lb_im.reshape(S5_NKT, S5_SL)], axis=1)
    lam = lam.reshape(2 * S5_NKT * (S5_SL // LANES), LANES)
    return wb, wc, lam, d_skip.astype(F32).reshape(1, D_MODEL)


def _s5_ln(x3, wb, wc, lam, dsk, wout, j, g, b, layer):
    bsz, seq, d = x3.shape
    t = min(TOK_T, seq)
    tok = pl.BlockSpec((1, t, d), lambda bi, i: (bi, i, 0))
    return pl.pallas_call(
        _s5_ln_kernel,
        grid=(bsz, seq // t),
        in_specs=[tok, _const_spec(wb.shape), _const_spec(wc.shape), _const_spec(lam.shape),
                  _const_spec(dsk.shape), _layer_spec(wout.shape, j), _layer_spec(g.shape, layer),
                  _layer_spec(b.shape, layer)],
        out_specs=tok,
        out_shape=jax.ShapeDtypeStruct((bsz, seq, d), F32),
        scratch_shapes=[pltpu.VMEM((S5_NKT, (S5_SL // LANES) * S5_PITCH, LANES), F32),
                        pltpu.VMEM((S5_NKT, (S5_SL // LANES) * S5_PITCH, LANES), F32),
                        pltpu.VMEM((t, d), F32), pltpu.VMEM(lam.shape, F32)],
        compiler_params=_params(2),
        name="s5_ln",
    )(x3, wb, wc, lam, dsk, wout, g, b)


def kernel(x, ffn1_w_in, ffn1_w_out, ln1_g, ln1_b, lnm_g, lnm_b, ffn2_w_in, ffn2_w_out, ln2_g, ln2_b, fox_w_in, fox_b_f, fox_w_o, s5_a_re, s5_a_im, s5_log_dt, s5_b_re, s5_b_im, s5_c_re, s5_c_im, s5_d, s5_w_out):
    bsz, seq, d = x.shape
    n = bsz * seq
    rows = lambda v: v.astype(F32).reshape(v.shape[0], 1, d)
    ffn1 = (ffn1_w_in.astype(BF16), ffn1_w_out.astype(BF16), rows(ln1_g), rows(ln1_b))
    ffn2 = (ffn2_w_in.astype(BF16), ffn2_w_out.astype(BF16), rows(ln2_g), rows(ln2_b))
    gm, bm = rows(lnm_g), rows(lnm_b)
    fox_wo = fox_w_o.astype(BF16)
    s5_wout = s5_w_out.astype(BF16)
    x2 = x.reshape(n, d)
    for i in range(DEPTH):
        x2 = _ffn_ln(x2, *ffn1, i)
        j = i // 2
        if i % 2 == 0:
            w = fox_w_in[j]
            scale = LOG2E / (HEAD_DIM ** 0.5)
            wqt = (w[:, :d] * scale).T.astype(BF16)
            wk = w[:, d:2 * d].astype(BF16)
            wvt = w[:, 2 * d:3 * d].T.astype(BF16)
            wf = jnp.pad(w[:, 3 * d:], ((0, 0), (0, LANES - FOX_HEADS))).astype(BF16)
            bf = jnp.pad(fox_b_f[j].astype(F32), (0, LANES - FOX_HEADS)).reshape(1, LANES)
            qt, k, vt, cqt, ck = _fox_proj(x2.reshape(bsz, seq, d), wqt, wk, wvt, wf, bf)
            a = _fox_attn(qt, cqt, k, ck, vt)
            x2 = _ffn_ln(x2, *ffn2, i, attn=(a.reshape(n, d), fox_wo, gm, bm, j))
        else:
            wb, wc, lam, dsk = _s5_tables(s5_a_re[j], s5_a_im[j], s5_log_dt[j], s5_b_re[j],
                                          s5_b_im[j], s5_c_re[j], s5_c_im[j], s5_d[j])
            x2 = _s5_ln(x2.reshape(bsz, seq, d), wb, wc, lam, dsk, s5_wout, j, gm, bm, i
                        ).reshape(n, d)
            x2 = _ffn_ln(x2, *ffn2, i)
    return x2.reshape(bsz, seq, d)
```

```python
import functools

import jax
import jax.numpy as jnp
import numpy as np
from jax import lax
from jax.experimental import pallas as pl
from jax.experimental.pallas import tpu as pltpu

F32 = jnp.float32
BF16 = jnp.bfloat16

D_MODEL = 1024
DEPTH = 4
FOX_HEADS = 16
HEAD_DIM = D_MODEL // FOX_HEADS
S5_GROUP = 16
S5_GROUPS = D_MODEL // S5_GROUP
S5_STATE = 64
D_FF = 2816
ALPHA = (2.0 * DEPTH) ** 0.25
LN_EPS = 1e-5
NEG_INF = -1e30

LANES = 128
MXU_DIM = 256
VMEM_LIMIT = 56 * 1024 * 1024

FFN_TM = 1024
FFN_PARTS = 2
FFN_FC = 256
TOK_T = 512
ATT_T = 512
S5_PITCH = TOK_T + 8
S5_KT = MXU_DIM
S5_NKT = D_MODEL // S5_KT
S5_SL = (S5_KT // S5_GROUP) * S5_STATE
C_PARTS = 3
DEN_ROWS = 16
LOG2E = 1.4426950408889634
HEAD_SLOT_SHIFT = 3
HEAD_SLOT = 1 << HEAD_SLOT_SHIFT


def _params(n_grid):
    return pltpu.CompilerParams(dimension_semantics=("arbitrary",) * n_grid,
                                vmem_limit_bytes=VMEM_LIMIT)


def _const_spec(shape):
    nd = len(shape)
    return pl.BlockSpec(shape, lambda *_: (0,) * nd, pipeline_mode=pl.Buffered(1))


def _layer_spec(shape, layer):
    nd = len(shape) - 1
    return pl.BlockSpec((None,) + tuple(shape[1:]), lambda *_: (layer,) + (0,) * nd,
                        pipeline_mode=pl.Buffered(1))


def _ln(y, g, b):
    mu = jnp.mean(y, axis=-1, keepdims=True)
    d = y - mu
    var = jnp.mean(d * d, axis=-1, keepdims=True)
    return d * lax.rsqrt(var + LN_EPS) * g + b


def _split3(c):
    hi = c.astype(BF16)
    r1 = c - hi.astype(F32)
    mid = r1.astype(BF16)
    lo = (r1 - mid.astype(F32)).astype(BF16)
    return hi, mid, lo


def _ffn_ln_kernel(*refs, with_proj):
    if with_proj:
        x_ref, a_ref, wo_ref, gm_ref, bm_ref, win_ref, wout_ref, g_ref, b_ref, o_ref, h_ref = refs
    else:
        x_ref, win_ref, wout_ref, g_ref, b_ref, o_ref, h_ref = refs
    part_rows = x_ref.shape[0] // FFN_PARTS
    for part in range(FFN_PARTS):
        rows = slice(part * part_rows, (part + 1) * part_rows)
        x = x_ref[rows, :]
        if with_proj:
            m = jnp.dot(a_ref[rows, :], wo_ref[...], preferred_element_type=F32)
            x = _ln(ALPHA * x + m, gm_ref[...], bm_ref[...])
        xb = x.astype(BF16)
        for c in range(D_FF // FFN_FC):
            lo = c * FFN_FC
            gate = jnp.dot(xb, win_ref[:, lo:lo + FFN_FC], preferred_element_type=F32)
            up = jnp.dot(xb, win_ref[:, D_FF + lo:D_FF + lo + FFN_FC], preferred_element_type=F32)
            h_ref[rows, lo:lo + FFN_FC] = (gate * jax.nn.sigmoid(gate) * up).astype(BF16)
        acc = jnp.dot(h_ref[rows, :], wout_ref[...], preferred_element_type=F32)
        o_ref[rows, :] = _ln(ALPHA * x + 0.5 * acc, g_ref[...], b_ref[...])


def _ffn_ln(x2, win, wout, g, b, layer, attn=None):
    n, d = x2.shape
    tm = min(FFN_TM, n)
    tile = pl.BlockSpec((tm, d), lambda i: (i, 0))
    args, specs = [x2], [tile]
    if attn is not None:
        a2, wo, gm, bm, j = attn
        args += [a2, wo, gm, bm]
        specs += [tile, _layer_spec(wo.shape, j), _layer_spec(gm.shape, layer),
                  _layer_spec(bm.shape, layer)]
    args += [win, wout, g, b]
    specs += [_layer_spec(a.shape, layer) for a in (win, wout, g, b)]
    return pl.pallas_call(
        functools.partial(_ffn_ln_kernel, with_proj=attn is not None),
        grid=(n // tm,),
        in_specs=specs,
        out_specs=tile,
        out_shape=jax.ShapeDtypeStruct((n, d), F32),
        scratch_shapes=[pltpu.VMEM((tm, D_FF), BF16)],
        compiler_params=_params(1),
        name="ffn_ln",
    )(*args)


def _fox_proj_kernel(x_ref, wqt_ref, wk_ref, wvt_ref, wf_ref, bf_ref, eqt_ref, ek_ref, oq_ref, ok_ref,
                     qt_ref, k_ref, vt_ref, cqt_ref, ck_ref, carry_ref):
    @pl.when(pl.program_id(1) == 0)
    def _():
        carry_ref[...] = jnp.zeros_like(carry_ref)

    xb = x_ref[0].astype(BF16)
    t = xb.shape[0]
    d = D_MODEL
    k_ref[0] = jnp.dot(xb, wk_ref[...], preferred_element_type=F32).astype(BF16)
    nt = (((1,), (1,)), ((), ()))
    qt_ref[0, 0] = lax.dot_general(wqt_ref[...], xb, nt, preferred_element_type=F32).astype(BF16)
    vt_ref[0, 0] = lax.dot_general(wvt_ref[...], xb, nt, preferred_element_type=F32).astype(BF16)

    f = jnp.dot(xb, wf_ref[...], preferred_element_type=F32) + bf_ref[...]
    logf = jnp.minimum(f, 0.0) - jnp.log1p(jnp.exp(-jnp.abs(f)))
    row = lax.broadcasted_iota(jnp.int32, (t, t), 0)
    col = lax.broadcasted_iota(jnp.int32, (t, t), 1)
    tri = (col <= row).astype(BF16)
    pieces = jnp.concatenate(_split3(logf), axis=1)
    cum3 = jnp.dot(tri, pieces, preferred_element_type=F32)
    c = (cum3[:, 0:LANES] + cum3[:, LANES:2 * LANES] + cum3[:, 2 * LANES:3 * LANES]
         + carry_ref[...])
    carry_ref[...] = c[t - 1:t, :]
    ccat = jnp.concatenate(_split3(c * LOG2E), axis=1)
    cqt_ref[0, 0] = (lax.dot_general(eqt_ref[...], ccat, nt, preferred_element_type=F32)
                     + oq_ref[...]).astype(BF16)
    ck_ref[0] = (jnp.dot(ccat, ek_ref[...], preferred_element_type=F32) + ok_ref[...]).astype(BF16)


def _bias_routing():
    eq = np.zeros((C_PARTS * LANES, LANES), np.float32)
    ek = np.zeros((C_PARTS * LANES, LANES), np.float32)
    oq = np.zeros((1, LANES), np.float32)
    ok = np.zeros((1, LANES), np.float32)
    h = np.arange(FOX_HEADS)
    for p in range(C_PARTS):
        eq[p * LANES + h, h * HEAD_SLOT + p] = 1.0
        ek[p * LANES + h, h * HEAD_SLOT + C_PARTS + p] = -1.0
        oq[0, h * HEAD_SLOT + C_PARTS + p] = 1.0
        ok[0, h * HEAD_SLOT + p] = 1.0
    return (jnp.asarray(eq.T, BF16), jnp.asarray(ek, BF16), jnp.asarray(oq.T), jnp.asarray(ok))


def _fox_proj(x3, wqt, wk, wvt, wf, bf):
    bsz, seq, d = x3.shape
    t = min(ATT_T, seq)
    eqt, ek, oq, ok = _bias_routing()
    tok = lambda w: pl.BlockSpec((1, t, w), lambda b, i: (b, i, 0))
    tr = lambda w: pl.BlockSpec((1, 1, w, t), lambda b, i: (b, i, 0, 0))
    big = jax.ShapeDtypeStruct((bsz, seq, d), BF16)
    big_t = jax.ShapeDtypeStruct((bsz, seq // t, d, t), BF16)
    small = jax.ShapeDtypeStruct((bsz, seq, LANES), BF16)
    return pl.pallas_call(
        _fox_proj_kernel,
        grid=(bsz, seq // t),
        in_specs=[tok(d), _const_spec(wqt.shape), _const_spec(wk.shape), _const_spec(wvt.shape),
                  _const_spec(wf.shape), _const_spec(bf.shape), _const_spec(eqt.shape),
                  _const_spec(ek.shape), _const_spec(oq.shape), _const_spec(ok.shape)],
        out_specs=[tr(d), tok(d), tr(d), tr(LANES), tok(LANES)],
        out_shape=[big_t, big, big_t, jax.ShapeDtypeStruct((bsz, seq // t, LANES, t), BF16), small],
        scratch_shapes=[pltpu.VMEM((1, LANES), F32)],
        compiler_params=_params(2),
        name="fox_proj",
    )(x3, wqt, wk, wvt, wf, bf, eqt, ek, oq, ok)


def _fox_attn_kernel(qt_ref, cqt_ref, k_ref, ck_ref, vt_ref, o_ref,
                     m_ref, acc_ref, sa_ref, sb_ref):
    pair = pl.program_id(1)
    n_q, _, t = qt_ref.shape[1:]
    feat = lax.broadcasted_iota(jnp.int32, (LANES, t), 0)
    key_pos = lax.broadcasted_iota(jnp.int32, (t, t), 0)
    qry_pos = lax.broadcasted_iota(jnp.int32, (t, t), 1)
    first_half = feat < HEAD_DIM
    ones_rows = jnp.ones((DEN_ROWS, t), BF16)

    def scores(j, lhs_t, s_ref):
        off = pl.multiple_of(j * t, t)
        rhs = jnp.concatenate([k_ref[0, pl.ds(off, t), :], ck_ref[0, pl.ds(off, t), :]], axis=1)
        for hh in range(2):
            s_ref[hh] = jnp.dot(rhs, lhs_t[hh], preferred_element_type=F32)

    def absorb(j, s_ref, masked):
        for hh in range(2):
            st = s_ref[hh]
            if masked:
                st = jnp.where(key_pos <= qry_pos, st, NEG_INF)
            m_prev = m_ref[hh]
            m_new = jnp.maximum(m_prev, jnp.max(st, axis=0, keepdims=True))
            pt = jnp.exp2(st - m_new)
            a = jnp.exp2(m_prev - m_new)
            vt = jnp.concatenate([vt_ref[0, j, hh * HEAD_DIM:(hh + 1) * HEAD_DIM, :], ones_rows],
                                 axis=0)
            acc_ref[hh] = a * acc_ref[hh] + jnp.dot(vt, pt.astype(BF16),
                                                    preferred_element_type=F32)
            m_ref[hh] = m_new

    for qi in range(n_q):
        qt = qt_ref[0, qi]
        cqt = cqt_ref[0, qi]
        lhs_t = []
        for hh in range(2):
            head = 2 * pair + hh
            qm = jnp.where(first_half if hh == 0 else jnp.logical_not(first_half), qt,
                           jnp.zeros_like(qt))
            cm = jnp.where((feat >> HEAD_SLOT_SHIFT) == head, cqt, jnp.zeros_like(cqt))
            lhs_t.append(jnp.concatenate([qm, cm], axis=0))
        m_ref[...] = jnp.full(m_ref.shape, NEG_INF, F32)
        acc_ref[...] = jnp.zeros(acc_ref.shape, F32)

        scores(0, lhs_t, sa_ref)

        def body(i, carry, lhs_t=lhs_t):
            j = 2 * i
            scores(j + 1, lhs_t, sb_ref)
            absorb(j, sa_ref, False)
            scores(j + 2, lhs_t, sa_ref)
            absorb(j + 1, sb_ref, False)
            return carry

        for i in range(qi // 2):
            body(i, 0)
        if qi % 2 == 0:
            absorb(qi, sa_ref, True)
        else:
            scores(qi, lhs_t, sb_ref)
            absorb(qi - 1, sa_ref, False)
            absorb(qi, sb_ref, True)

        ot = jnp.concatenate([acc_ref[hh, :HEAD_DIM, :] / acc_ref[hh, HEAD_DIM:HEAD_DIM + 1, :]
                              for hh in range(2)], axis=0)
        o_ref[0, qi * t:(qi + 1) * t, :] = ot.T.astype(BF16)


def _fox_attn(qt, cqt, k, ck, vt):
    bsz, seq, d = k.shape
    t = min(ATT_T, seq)
    n_q = seq // t
    return pl.pallas_call(
        _fox_attn_kernel,
        grid=(bsz, d // LANES),
        in_specs=[pl.BlockSpec((1, n_q, LANES, t), lambda b, p: (b, 0, p, 0)),
                  pl.BlockSpec((1, n_q, LANES, t), lambda b, p: (b, 0, 0, 0)),
                  pl.BlockSpec((1, seq, LANES), lambda b, p: (b, 0, p)),
                  pl.BlockSpec((1, seq, LANES), lambda b, p: (b, 0, 0)),
                  pl.BlockSpec((1, n_q, LANES, t), lambda b, p: (b, 0, p, 0))],
        out_specs=pl.BlockSpec((1, seq, LANES), lambda b, p: (b, 0, p)),
        out_shape=jax.ShapeDtypeStruct((bsz, seq, d), BF16),
        scratch_shapes=[pltpu.VMEM((2, 1, t), F32),
                        pltpu.VMEM((2, HEAD_DIM + DEN_ROWS, t), F32),
                        pltpu.VMEM((2, t, t), F32), pltpu.VMEM((2, t, t), F32)],
        compiler_params=_params(2),
        name="fox_attn",
    )(qt, cqt, k, ck, vt)


def _cmul(ar, ai, br, bi):
    return ar * br - ai * bi, ar * bi + ai * br


def _s5_ln_kernel(x_ref, wb_ref, wc_ref, lam_ref, d_ref, wout_ref, g_ref, b_ref, o_ref,
                  sa_ref, sb_ref, y_ref, carry_ref):
    @pl.when(pl.program_id(1) == 0)
    def _():
        carry_ref[...] = jnp.zeros_like(carry_ref)

    x = x_ref[0]
    xb = x.astype(BF16)
    t = x.shape[0]
    nb = S5_SL // LANES
    per_half = S5_NKT // 2
    slab = lambda j: slice(j * S5_PITCH, j * S5_PITCH + t)
    halves = [(sa_ref, range(0, per_half)), (sb_ref, range(per_half, S5_NKT))]

    def project_in(s_ref, kts):
        for i, kt in enumerate(kts):
            cols = slice(kt * S5_KT, (kt + 1) * S5_KT)
            bu = jnp.dot(xb[:, cols], wb_ref[kt], preferred_element_type=F32)
            for j in range(nb):
                s_ref[2 * i, slab(j), :] = bu[:, j * LANES:(j + 1) * LANES]
                s_ref[2 * i + 1, slab(j), :] = bu[:, S5_SL + j * LANES:S5_SL + (j + 1) * LANES]

    def recur(s_ref, kts):
        rows = lambda kt, part: slice((2 * kt + part) * nb, (2 * kt + part + 1) * nb)
        lam = [(lam_ref[rows(kt, 0), :], lam_ref[rows(kt, 1), :]) for kt in kts]
        h = [(carry_ref[rows(kt, 0), :], carry_ref[rows(kt, 1), :]) for kt in kts]
        for ti in range(t):
            now = pl.ds(ti, nb, stride=S5_PITCH)
            for i in range(len(h)):
                mr, mi = _cmul(lam[i][0], lam[i][1], h[i][0], h[i][1])
                hr = mr + s_ref[2 * i, now, :]
                hi = mi + s_ref[2 * i + 1, now, :]
                s_ref[2 * i, now, :] = hr
                s_ref[2 * i + 1, now, :] = hi
                h[i] = (hr, hi)
        for i, kt in enumerate(kts):
            carry_ref[rows(kt, 0), :] = h[i][0]
            carry_ref[rows(kt, 1), :] = h[i][1]

    def project_out(s_ref, kts):
        for i, kt in enumerate(kts):
            cols = slice(kt * S5_KT, (kt + 1) * S5_KT)
            hcat = jnp.concatenate([s_ref[2 * i + part, slab(j), :].astype(BF16)
                                    for part in range(2) for j in range(nb)], axis=1)
            y_ref[:, cols] = (jnp.dot(hcat, wc_ref[kt], preferred_element_type=F32)
                              + d_ref[:, cols] * x[:, cols])

    (s_a, kts_a), (s_b, kts_b) = halves
    project_in(s_a, kts_a)
    project_in(s_b, kts_b)
    recur(s_a, kts_a)

    @pl.when(pl.program_id(1) >= 0)
    def _():
        recur(s_b, kts_b)
        project_out(s_a, kts_a)

    project_out(s_b, kts_b)
    ge = jax.nn.gelu(y_ref[...]).astype(BF16)
    vg = jnp.dot(ge, wout_ref[...], preferred_element_type=F32)
    m = vg[:, :D_MODEL] * jax.nn.sigmoid(vg[:, D_MODEL:])
    o_ref[0] = _ln(ALPHA * x + m, g_ref[...], b_ref[...])


def _s5_tables(a_re, a_im, log_dt, b_re, b_im, c_re, c_im, d_skip):
    lr = a_re.astype(F32)
    li = a_im.astype(F32)
    dt = jnp.exp(log_dt.astype(F32))[:, None]
    mag = jnp.exp(lr * dt)
    ang = li * dt
    lb_re = mag * jnp.cos(ang)
    lb_im = mag * jnp.sin(ang)
    den = lr * lr + li * li
    nr = lb_re - 1.0
    ni = lb_im
    z_re = (nr * lr + ni * li) / den
    z_im = (ni * lr - nr * li) / den
    br = b_re.astype(F32)
    bi = b_im.astype(F32)
    bb_re = z_re[..., None] * br - z_im[..., None] * bi
    bb_im = z_re[..., None] * bi + z_im[..., None] * br
    gk = S5_KT // S5_GROUP

    def blockdiag(m, width):
        a, b = m.shape[1:]
        tiled = jnp.tile(m.reshape(S5_NKT, gk, a, b), (1, 1, 1, gk))
        on_diag = (np.arange(gk * b) // b)[None, :] == np.arange(gk)[:, None]
        return jnp.where(on_diag[None, :, None, :], tiled, 0.0).reshape(S5_NKT, gk * a, width)

    blockdiag_b = lambda m: blockdiag(jnp.swapaxes(m, 1, 2), S5_SL)
    blockdiag_c = lambda m: blockdiag(jnp.swapaxes(m, 1, 2), S5_KT)

    wb = jnp.concatenate([blockdiag_b(bb_re), blockdiag_b(bb_im)], axis=2).astype(BF16)
    wc = jnp.concatenate([blockdiag_c(c_re.astype(F32)), -blockdiag_c(c_im.astype(F32))],
                         axis=1).astype(BF16)
    lam = jnp.stack([lb_re.reshape(S5_NKT, S5_SL), lb_im.reshape(S5_NKT, S5_SL)], axis=1)
    lam = lam.reshape(2 * S5_NKT * (S5_SL // LANES), LANES)
    return wb, wc, lam, d_skip.astype(F32).reshape(1, D_MODEL)


def _s5_ln(x3, wb, wc, lam, dsk, wout, j, g, b, layer):
    bsz, seq, d = x3.shape
    t = min(TOK_T, seq)
    tok = pl.BlockSpec((1, t, d), lambda bi, i: (bi, i, 0))
    return pl.pallas_call(
        _s5_ln_kernel,
        grid=(bsz, seq // t),
        in_specs=[tok, _const_spec(wb.shape), _const_spec(wc.shape), _const_spec(lam.shape),
                  _const_spec(dsk.shape), _layer_spec(wout.shape, j), _layer_spec(g.shape, layer),
                  _layer_spec(b.shape, layer)],
        out_specs=tok,
        out_shape=jax.ShapeDtypeStruct((bsz, seq, d), F32),
        scratch_shapes=[pltpu.VMEM((S5_NKT, (S5_SL // LANES) * S5_PITCH, LANES), F32),
                        pltpu.VMEM((S5_NKT, (S5_SL // LANES) * S5_PITCH, LANES), F32),
                        pltpu.VMEM((t, d), F32), pltpu.VMEM(lam.shape, F32)],
        compiler_params=_params(2),
        name="s5_ln",
    )(x3, wb, wc, lam, dsk, wout, g, b)


def kernel(x, ffn1_w_in, ffn1_w_out, ln1_g, ln1_b, lnm_g, lnm_b, ffn2_w_in, ffn2_w_out, ln2_g, ln2_b, fox_w_in, fox_b_f, fox_w_o, s5_a_re, s5_a_im, s5_log_dt, s5_b_re, s5_b_im, s5_c_re, s5_c_im, s5_d, s5_w_out):
    bsz, seq, d = x.shape
    n = bsz * seq
    rows = lambda v: v.astype(F32).reshape(v.shape[0], 1, d)
    ffn1 = (ffn1_w_in.astype(BF16), ffn1_w_out.astype(BF16), rows(ln1_g), rows(ln1_b))
    ffn2 = (ffn2_w_in.astype(BF16), ffn2_w_out.astype(BF16), rows(ln2_g), rows(ln2_b))
    gm, bm = rows(lnm_g), rows(lnm_b)
    fox_wo = fox_w_o.astype(BF16)
    s5_wout = s5_w_out.astype(BF16)
    x2 = x.reshape(n, d)
    for i in range(DEPTH):
        x2 = _ffn_ln(x2, *ffn1, i)
        j = i // 2
        if i % 2 == 0:
            w = fox_w_in[j]
            scale = LOG2E / (HEAD_DIM ** 0.5)
            wqt = (w[:, :d] * scale).T.astype(BF16)
            wk = w[:, d:2 * d].astype(BF16)
            wvt = w[:, 2 * d:3 * d].T.astype(BF16)
            wf = jnp.pad(w[:, 3 * d:], ((0, 0), (0, LANES - FOX_HEADS))).astype(BF16)
            bf = jnp.pad(fox_b_f[j].astype(F32), (0, LANES - FOX_HEADS)).reshape(1, LANES)
            qt, k, vt, cqt, ck = _fox_proj(x2.reshape(bsz, seq, d), wqt, wk, wvt, wf, bf)
            a = _fox_attn(qt, cqt, k, ck, vt)
            x2 = _ffn_ln(x2, *ffn2, i, attn=(a.reshape(n, d), fox_wo, gm, bm, j))
        else:
            wb, wc, lam, dsk = _s5_tables(s5_a_re[j], s5_a_im[j], s5_log_dt[j], s5_b_re[j],
                                          s5_b_im[j], s5_c_re[j], s5_c_im[j], s5_d[j])
            x2 = _s5_ln(x2.reshape(bsz, seq, d), wb, wc, lam, dsk, s5_wout, j, gm, bm, i
                        ).reshape(n, d)
            x2 = _ffn_ln(x2, *ffn2, i)
    return x2.reshape(bsz, seq, d)
```

```python
import functools

import jax
import jax.numpy as jnp
import numpy as np
from jax import lax
from jax.experimental import pallas as pl
from jax.experimental.pallas import tpu as pltpu

F32 = jnp.float32
BF16 = jnp.bfloat16

D_MODEL = 1024
DEPTH = 4
FOX_HEADS = 16
HEAD_DIM = D_MODEL // FOX_HEADS
S5_GROUP = 16
S5_GROUPS = D_MODEL // S5_GROUP
S5_STATE = 64
D_FF = 2816
ALPHA = (2.0 * DEPTH) ** 0.25
LN_EPS = 1e-5
NEG_INF = -1e30

LANES = 128
MXU_DIM = 256
VMEM_LIMIT = 56 * 1024 * 1024

FFN_TM = 1024
FFN_PARTS = 2
FFN_FC = 256
FFN_WIN_CHUNK = 64
FFN_WOUT_CHUNK = 352
TOK_T = 512
ATT_T = 512
S5_PITCH = TOK_T + 8
S5_KT = MXU_DIM
S5_NKT = D_MODEL // S5_KT
S5_SL = (S5_KT // S5_GROUP) * S5_STATE
C_PARTS = 3
DEN_ROWS = 16
LOG2E = 1.4426950408889634
HEAD_SLOT_SHIFT = 3
HEAD_SLOT = 1 << HEAD_SLOT_SHIFT


def _params(n_grid):
    return pltpu.CompilerParams(dimension_semantics=("arbitrary",) * n_grid,
                                vmem_limit_bytes=VMEM_LIMIT)


def _const_spec(shape):
    nd = len(shape)
    return pl.BlockSpec(shape, lambda *_: (0,) * nd, pipeline_mode=pl.Buffered(1))


def _layer_spec(shape, layer):
    nd = len(shape) - 1
    return pl.BlockSpec((None,) + tuple(shape[1:]), lambda *_: (layer,) + (0,) * nd,
                        pipeline_mode=pl.Buffered(1))


def _ln(y, g, b):
    mu = jnp.mean(y, axis=-1, keepdims=True)
    d = y - mu
    var = jnp.mean(d * d, axis=-1, keepdims=True)
    return d * lax.rsqrt(var + LN_EPS) * g + b


def _split3(c):
    hi = c.astype(BF16)
    r1 = c - hi.astype(F32)
    mid = r1.astype(BF16)
    lo = (r1 - mid.astype(F32)).astype(BF16)
    return hi, mid, lo


def _load_as_bf16(src_hbm, layer, dst_ref, stage_ref, sem_ref, rows_per_chunk):
    n_chunks = dst_ref.shape[0] // rows_per_chunk

    def chunk_copy(c):
        return pltpu.make_async_copy(
            src_hbm.at[layer, pl.ds(c * rows_per_chunk, rows_per_chunk), :],
            stage_ref.at[c % 2], sem_ref.at[c % 2])

    chunk_copy(0).start()
    for c in range(n_chunks):
        if c + 1 < n_chunks:
            chunk_copy(c + 1).start()
        chunk_copy(c).wait()
        dst_ref[c * rows_per_chunk:(c + 1) * rows_per_chunk, :] = stage_ref[c % 2].astype(BF16)


def _ffn_ln_kernel(*refs, with_proj, layer):
    if with_proj:
        (x_ref, a_ref, wo_ref, gm_ref, bm_ref, win_hbm, wout_hbm, g_ref, b_ref, o_ref,
         h_ref, win_ref, wout_ref, stage_in, stage_out, sem_in, sem_out) = refs
    else:
        (x_ref, win_hbm, wout_hbm, g_ref, b_ref, o_ref,
         h_ref, win_ref, wout_ref, stage_in, stage_out, sem_in, sem_out) = refs

    @pl.when(pl.program_id(0) == 0)
    def _():
        _load_as_bf16(win_hbm, layer, win_ref, stage_in, sem_in, FFN_WIN_CHUNK)
        _load_as_bf16(wout_hbm, layer, wout_ref, stage_out, sem_out, FFN_WOUT_CHUNK)

    part_rows = x_ref.shape[0] // FFN_PARTS
    for part in range(FFN_PARTS):
        rows = slice(part * part_rows, (part + 1) * part_rows)
        x = x_ref[rows, :]
        if with_proj:
            m = jnp.dot(a_ref[rows, :], wo_ref[...], preferred_element_type=F32)
            x = _ln(ALPHA * x + m, gm_ref[...], bm_ref[...])
        xb = x.astype(BF16)
        for c in range(D_FF // FFN_FC):
            lo = c * FFN_FC
            gate = jnp.dot(xb, win_ref[:, lo:lo + FFN_FC], preferred_element_type=F32)
            up = jnp.dot(xb, win_ref[:, D_FF + lo:D_FF + lo + FFN_FC], preferred_element_type=F32)
            h_ref[rows, lo:lo + FFN_FC] = (gate * jax.nn.sigmoid(gate) * up).astype(BF16)
        acc = jnp.dot(h_ref[rows, :], wout_ref[...], preferred_element_type=F32)
        o_ref[rows, :] = _ln(ALPHA * x + 0.5 * acc, g_ref[...], b_ref[...])


def _ffn_ln(x2, win, wout, g, b, layer, attn=None):
    n, d = x2.shape
    tm = min(FFN_TM, n)
    tile = pl.BlockSpec((tm, d), lambda i: (i, 0))
    hbm = pl.BlockSpec(memory_space=pl.ANY)
    args, specs = [x2], [tile]
    if attn is not None:
        a2, wo, gm, bm, j = attn
        args += [a2, wo, gm, bm]
        specs += [tile, _layer_spec(wo.shape, j), _layer_spec(gm.shape, layer),
                  _layer_spec(bm.shape, layer)]
    args += [win, wout, g, b]
    specs += [hbm, hbm, _layer_spec(g.shape, layer), _layer_spec(b.shape, layer)]
    return pl.pallas_call(
        functools.partial(_ffn_ln_kernel, with_proj=attn is not None, layer=layer),
        grid=(n // tm,),
        in_specs=specs,
        out_specs=tile,
        out_shape=jax.ShapeDtypeStruct((n, d), F32),
        scratch_shapes=[pltpu.VMEM((tm, D_FF), BF16),
                        pltpu.VMEM(win.shape[1:], BF16), pltpu.VMEM(wout.shape[1:], BF16),
                        pltpu.VMEM((2, FFN_WIN_CHUNK, win.shape[2]), F32),
                        pltpu.VMEM((2, FFN_WOUT_CHUNK, wout.shape[2]), F32),
                        pltpu.SemaphoreType.DMA((2,)), pltpu.SemaphoreType.DMA((2,))],
        compiler_params=_params(1),
        name="ffn_ln",
    )(*args)


def _fox_proj_kernel(x_ref, wqt_ref, wk_ref, wvt_ref, wf_ref, bf_ref, eqt_ref, ek_ref, oq_ref, ok_ref,
                     qt_ref, k_ref, vt_ref, cqt_ref, ck_ref, carry_ref):
    @pl.when(pl.program_id(1) == 0)
    def _():
        carry_ref[...] = jnp.zeros_like(carry_ref)

    xb = x_ref[0].astype(BF16)
    t = xb.shape[0]
    d = D_MODEL
    k_ref[0] = jnp.dot(xb, wk_ref[...], preferred_element_type=F32).astype(BF16)
    nt = (((1,), (1,)), ((), ()))
    qt_ref[0, 0] = lax.dot_general(wqt_ref[...], xb, nt, preferred_element_type=F32).astype(BF16)
    vt_ref[0, 0] = lax.dot_general(wvt_ref[...], xb, nt, preferred_element_type=F32).astype(BF16)

    f = jnp.dot(xb, wf_ref[...], preferred_element_type=F32) + bf_ref[...]
    logf = jnp.minimum(f, 0.0) - jnp.log1p(jnp.exp(-jnp.abs(f)))
    row = lax.broadcasted_iota(jnp.int32, (t, t), 0)
    col = lax.broadcasted_iota(jnp.int32, (t, t), 1)
    tri = (col <= row).astype(BF16)
    pieces = jnp.concatenate(_split3(logf), axis=1)
    cum3 = jnp.dot(tri, pieces, preferred_element_type=F32)
    c = (cum3[:, 0:LANES] + cum3[:, LANES:2 * LANES] + cum3[:, 2 * LANES:3 * LANES]
         + carry_ref[...])
    carry_ref[...] = c[t - 1:t, :]
    ccat = jnp.concatenate(_split3(c * LOG2E), axis=1)
    cqt_ref[0, 0] = (lax.dot_general(eqt_ref[...], ccat, nt, preferred_element_type=F32)
                     + oq_ref[...]).astype(BF16)
    ck_ref[0] = (jnp.dot(ccat, ek_ref[...], preferred_element_type=F32) + ok_ref[...]).astype(BF16)


def _bias_routing():
    eq = np.zeros((C_PARTS * LANES, LANES), np.float32)
    ek = np.zeros((C_PARTS * LANES, LANES), np.float32)
    oq = np.zeros((1, LANES), np.float32)
    ok = np.zeros((1, LANES), np.float32)
    h = np.arange(FOX_HEADS)
    for p in range(C_PARTS):
        eq[p * LANES + h, h * HEAD_SLOT + p] = 1.0
        ek[p * LANES + h, h * HEAD_SLOT + C_PARTS + p] = -1.0
        oq[0, h * HEAD_SLOT + C_PARTS + p] = 1.0
        ok[0, h * HEAD_SLOT + p] = 1.0
    return (jnp.asarray(eq.T, BF16), jnp.asarray(ek, BF16), jnp.asarray(oq.T), jnp.asarray(ok))


def _fox_proj(x3, wqt, wk, wvt, wf, bf):
    bsz, seq, d = x3.shape
    t = min(ATT_T, seq)
    eqt, ek, oq, ok = _bias_routing()
    tok = lambda w: pl.BlockSpec((1, t, w), lambda b, i: (b, i, 0))
    tr = lambda w: pl.BlockSpec((1, 1, w, t), lambda b, i: (b, i, 0, 0))
    big = jax.ShapeDtypeStruct((bsz, seq, d), BF16)
    big_t = jax.ShapeDtypeStruct((bsz, seq // t, d, t), BF16)
    small = jax.ShapeDtypeStruct((bsz, seq, LANES), BF16)
    return pl.pallas_call(
        _fox_proj_kernel,
        grid=(bsz, seq // t),
        in_specs=[tok(d), _const_spec(wqt.shape), _const_spec(wk.shape), _const_spec(wvt.shape),
                  _const_spec(wf.shape), _const_spec(bf.shape), _const_spec(eqt.shape),
                  _const_spec(ek.shape), _const_spec(oq.shape), _const_spec(ok.shape)],
        out_specs=[tr(d), tok(d), tr(d), tr(LANES), tok(LANES)],
        out_shape=[big_t, big, big_t, jax.ShapeDtypeStruct((bsz, seq // t, LANES, t), BF16), small],
        scratch_shapes=[pltpu.VMEM((1, LANES), F32)],
        compiler_params=_params(2),
        name="fox_proj",
    )(x3, wqt, wk, wvt, wf, bf, eqt, ek, oq, ok)


def _fox_attn_kernel(qt_ref, cqt_ref, k_ref, ck_ref, vt_ref, o_ref,
                     m_ref, acc_ref, sa_ref, sb_ref):
    pair = pl.program_id(1)
    n_q, _, t = qt_ref.shape[1:]
    feat = lax.broadcasted_iota(jnp.int32, (LANES, t), 0)
    key_pos = lax.broadcasted_iota(jnp.int32, (t, t), 0)
    qry_pos = lax.broadcasted_iota(jnp.int32, (t, t), 1)
    first_half = feat < HEAD_DIM
    ones_rows = jnp.ones((DEN_ROWS, t), BF16)

    def scores(j, lhs_t, s_ref):
        off = pl.multiple_of(j * t, t)
        rhs = jnp.concatenate([k_ref[0, pl.ds(off, t), :], ck_ref[0, pl.ds(off, t), :]], axis=1)
        for hh in range(2):
            s_ref[hh] = jnp.dot(rhs, lhs_t[hh], preferred_element_type=F32)

    def absorb(j, s_ref, masked):
        for hh in range(2):
            st = s_ref[hh]
            if masked:
                st = jnp.where(key_pos <= qry_pos, st, NEG_INF)
            m_prev = m_ref[hh]
            m_new = jnp.maximum(m_prev, jnp.max(st, axis=0, keepdims=True))
            pt = jnp.exp2(st - m_new)
            a = jnp.exp2(m_prev - m_new)
            vt = jnp.concatenate([vt_ref[0, j, hh * HEAD_DIM:(hh + 1) * HEAD_DIM, :], ones_rows],
                                 axis=0)
            acc_ref[hh] = a * acc_ref[hh] + jnp.dot(vt, pt.astype(BF16),
                                                    preferred_element_type=F32)
            m_ref[hh] = m_new

    for qi in range(n_q):
        qt = qt_ref[0, qi]
        cqt = cqt_ref[0, qi]
        lhs_t = []
        for hh in range(2):
            head = 2 * pair + hh
            qm = jnp.where(first_half if hh == 0 else jnp.logical_not(first_half), qt,
                           jnp.zeros_like(qt))
            cm = jnp.where((feat >> HEAD_SLOT_SHIFT) == head, cqt, jnp.zeros_like(cqt))
            lhs_t.append(jnp.concatenate([qm, cm], axis=0))
        m_ref[...] = jnp.full(m_ref.shape, NEG_INF, F32)
        acc_ref[...] = jnp.zeros(acc_ref.shape, F32)

        scores(0, lhs_t, sa_ref)

        def body(i, carry, lhs_t=lhs_t):
            j = 2 * i
            scores(j + 1, lhs_t, sb_ref)
            absorb(j, sa_ref, False)
            scores(j + 2, lhs_t, sa_ref)
            absorb(j + 1, sb_ref, False)
            return carry

        for i in range(qi // 2):
            body(i, 0)
        if qi % 2 == 0:
            absorb(qi, sa_ref, True)
        else:
            scores(qi, lhs_t, sb_ref)
            absorb(qi - 1, sa_ref, False)
            absorb(qi, sb_ref, True)

        ot = jnp.concatenate([acc_ref[hh, :HEAD_DIM, :] / acc_ref[hh, HEAD_DIM:HEAD_DIM + 1, :]
                              for hh in range(2)], axis=0)
        o_ref[0, qi * t:(qi + 1) * t, :] = ot.T.astype(BF16)


def _fox_attn(qt, cqt, k, ck, vt):
    bsz, seq, d = k.shape
    t = min(ATT_T, seq)
    n_q = seq // t
    return pl.pallas_call(
        _fox_attn_kernel,
        grid=(bsz, d // LANES),
        in_specs=[pl.BlockSpec((1, n_q, LANES, t), lambda b, p: (b, 0, p, 0)),
                  pl.BlockSpec((1, n_q, LANES, t), lambda b, p: (b, 0, 0, 0)),
                  pl.BlockSpec((1, seq, LANES), lambda b, p: (b, 0, p)),
                  pl.BlockSpec((1, seq, LANES), lambda b, p: (b, 0, 0)),
                  pl.BlockSpec((1, n_q, LANES, t), lambda b, p: (b, 0, p, 0))],
        out_specs=pl.BlockSpec((1, seq, LANES), lambda b, p: (b, 0, p)),
        out_shape=jax.ShapeDtypeStruct((bsz, seq, d), BF16),
        scratch_shapes=[pltpu.VMEM((2, 1, t), F32),
                        pltpu.VMEM((2, HEAD_DIM + DEN_ROWS, t), F32),
                        pltpu.VMEM((2, t, t), F32), pltpu.VMEM((2, t, t), F32)],
        compiler_params=_params(2),
        name="fox_attn",
    )(qt, cqt, k, ck, vt)


def _cmul(ar, ai, br, bi):
    return ar * br - ai * bi, ar * bi + ai * br


def _s5_ln_kernel(x_ref, wb_ref, wc_ref, lam_ref, d_ref, wout_ref, g_ref, b_ref, o_ref,
                  sa_ref, sb_ref, y_ref, carry_ref):
    @pl.when(pl.program_id(1) == 0)
    def _():
        carry_ref[...] = jnp.zeros_like(carry_ref)

    x = x_ref[0]
    xb = x.astype(BF16)
    t = x.shape[0]
    nb = S5_SL // LANES
    per_half = S5_NKT // 2
    slab = lambda j: slice(j * S5_PITCH, j * S5_PITCH + t)
    halves = [(sa_ref, range(0, per_half)), (sb_ref, range(per_half, S5_NKT))]

    def project_in(s_ref, kts):
        for i, kt in enumerate(kts):
            cols = slice(kt * S5_KT, (kt + 1) * S5_KT)
            bu = jnp.dot(xb[:, cols], wb_ref[kt], preferred_element_type=F32)
            for j in range(nb):
                s_ref[2 * i, slab(j), :] = bu[:, j * LANES:(j + 1) * LANES]
                s_ref[2 * i + 1, slab(j), :] = bu[:, S5_SL + j * LANES:S5_SL + (j + 1) * LANES]

    def recur(s_ref, kts):
        rows = lambda kt, part: slice((2 * kt + part) * nb, (2 * kt + part + 1) * nb)
        lam = [(lam_ref[rows(kt, 0), :], lam_ref[rows(kt, 1), :]) for kt in kts]
        h = [(carry_ref[rows(kt, 0), :], carry_ref[rows(kt, 1), :]) for kt in kts]
        for ti in range(t):
            now = pl.ds(ti, nb, stride=S5_PITCH)
            for i in range(len(h)):
                mr, mi = _cmul(lam[i][0], lam[i][1], h[i][0], h[i][1])
                hr = mr + s_ref[2 * i, now, :]
                hi = mi + s_ref[2 * i + 1, now, :]
                s_ref[2 * i, now, :] = hr
                s_ref[2 * i + 1, now, :] = hi
                h[i] = (hr, hi)
        for i, kt in enumerate(kts):
            carry_ref[rows(kt, 0), :] = h[i][0]
            carry_ref[rows(kt, 1), :] = h[i][1]

    def project_out(s_ref, kts):
        for i, kt in enumerate(kts):
            cols = slice(kt * S5_KT, (kt + 1) * S5_KT)
            hcat = jnp.concatenate([s_ref[2 * i + part, slab(j), :].astype(BF16)
                                    for part in range(2) for j in range(nb)], axis=1)
            y_ref[:, cols] = (jnp.dot(hcat, wc_ref[kt], preferred_element_type=F32)
                              + d_ref[:, cols] * x[:, cols])

    (s_a, kts_a), (s_b, kts_b) = halves
    project_in(s_a, kts_a)
    project_in(s_b, kts_b)
    recur(s_a, kts_a)

    @pl.when(pl.program_id(1) >= 0)
    def _():
        recur(s_b, kts_b)
        project_out(s_a, kts_a)

    project_out(s_b, kts_b)
    ge = jax.nn.gelu(y_ref[...]).astype(BF16)
    vg = jnp.dot(ge, wout_ref[...], preferred_element_type=F32)
    m = vg[:, :D_MODEL] * jax.nn.sigmoid(vg[:, D_MODEL:])
    o_ref[0] = _ln(ALPHA * x + m, g_ref[...], b_ref[...])


def _s5_tables(a_re, a_im, log_dt, b_re, b_im, c_re, c_im, d_skip):
    lr = a_re.astype(F32)
    li = a_im.astype(F32)
    dt = jnp.exp(log_dt.astype(F32))[:, None]
    mag = jnp.exp(lr * dt)
    ang = li * dt
    lb_re = mag * jnp.cos(ang)
    lb_im = mag * jnp.sin(ang)
    den = lr * lr + li * li
    nr = lb_re - 1.0
    ni = lb_im
    z_re = (nr * lr + ni * li) / den
    z_im = (ni * lr - nr * li) / den
    br = b_re.astype(F32)
    bi = b_im.astype(F32)
    bb_re = z_re[..., None] * br - z_im[..., None] * bi
    bb_im = z_re[..., None] * bi + z_im[..., None] * br
    gk = S5_KT // S5_GROUP

    def blockdiag(m, width):
        a, b = m.shape[1:]
        tiled = jnp.tile(m.reshape(S5_NKT, gk, a, b), (1, 1, 1, gk))
        on_diag = (np.arange(gk * b) // b)[None, :] == np.arange(gk)[:, None]
        return jnp.where(on_diag[None, :, None, :], tiled, 0.0).reshape(S5_NKT, gk * a, width)

    blockdiag_b = lambda m: blockdiag(jnp.swapaxes(m, 1, 2), S5_SL)
    blockdiag_c = lambda m: blockdiag(jnp.swapaxes(m, 1, 2), S5_KT)

    wb = jnp.concatenate([blockdiag_b(bb_re), blockdiag_b(bb_im)], axis=2).astype(BF16)
    wc = jnp.concatenate([blockdiag_c(c_re.astype(F32)), -blockdiag_c(c_im.astype(F32))],
                         axis=1).astype(BF16)
    lam = jnp.stack([lb_re.reshape(S5_NKT, S5_SL), lb_im.reshape(S5_NKT, S5_SL)], axis=1)
    lam = lam.reshape(2 * S5_NKT * (S5_SL // LANES), LANES)
    return wb, wc, lam, d_skip.astype(F32).reshape(1, D_MODEL)


def _s5_ln(x3, wb, wc, lam, dsk, wout, j, g, b, layer):
    bsz, seq, d = x3.shape
    t = min(TOK_T, seq)
    tok = pl.BlockSpec((1, t, d), lambda bi, i: (bi, i, 0))
    return pl.pallas_call(
        _s5_ln_kernel,
        grid=(bsz, seq // t),
        in_specs=[tok, _const_spec(wb.shape), _const_spec(wc.shape), _const_spec(lam.shape),
                  _const_spec(dsk.shape), _layer_spec(wout.shape, j), _layer_spec(g.shape, layer),
                  _layer_spec(b.shape, layer)],
        out_specs=tok,
        out_shape=jax.ShapeDtypeStruct((bsz, seq, d), F32),
        scratch_shapes=[pltpu.VMEM((S5_NKT, (S5_SL // LANES) * S5_PITCH, LANES), F32),
                        pltpu.VMEM((S5_NKT, (S5_SL // LANES) * S5_PITCH, LANES), F32),
                        pltpu.VMEM((t, d), F32), pltpu.VMEM(lam.shape, F32)],
        compiler_params=_params(2),
        name="s5_ln",
    )(x3, wb, wc, lam, dsk, wout, g, b)


def kernel(x, ffn1_w_in, ffn1_w_out, ln1_g, ln1_b, lnm_g, lnm_b, ffn2_w_in, ffn2_w_out, ln2_g, ln2_b, fox_w_in, fox_b_f, fox_w_o, s5_a_re, s5_a_im, s5_log_dt, s5_b_re, s5_b_im, s5_c_re, s5_c_im, s5_d, s5_w_out):
    bsz, seq, d = x.shape
    n = bsz * seq
    rows = lambda v: v.astype(F32).reshape(v.shape[0], 1, d)
    ffn1 = (ffn1_w_in.astype(F32), ffn1_w_out.astype(F32), rows(ln1_g), rows(ln1_b))
    ffn2 = (ffn2_w_in.astype(F32), ffn2_w_out.astype(F32), rows(ln2_g), rows(ln2_b))
    gm, bm = rows(lnm_g), rows(lnm_b)
    fox_wo = fox_w_o.astype(BF16)
    s5_wout = s5_w_out.astype(BF16)
    x2 = x.reshape(n, d)
    for i in range(DEPTH):
        x2 = _ffn_ln(x2, *ffn1, i)
        j = i // 2
        if i % 2 == 0:
            w = fox_w_in[j]
            scale = LOG2E / (HEAD_DIM ** 0.5)
            wqt = (w[:, :d] * scale).T.astype(BF16)
            wk = w[:, d:2 * d].astype(BF16)
            wvt = w[:, 2 * d:3 * d].T.astype(BF16)
            wf = jnp.pad(w[:, 3 * d:], ((0, 0), (0, LANES - FOX_HEADS))).astype(BF16)
            bf = jnp.pad(fox_b_f[j].astype(F32), (0, LANES - FOX_HEADS)).reshape(1, LANES)
            qt, k, vt, cqt, ck = _fox_proj(x2.reshape(bsz, seq, d), wqt, wk, wvt, wf, bf)
            a = _fox_attn(qt, cqt, k, ck, vt)
            x2 = _ffn_ln(x2, *ffn2, i, attn=(a.reshape(n, d), fox_wo, gm, bm, j))
        else:
            wb, wc, lam, dsk = _s5_tables(s5_a_re[j], s5_a_im[j], s5_log_dt[j], s5_b_re[j],
                                          s5_b_im[j], s5_c_re[j], s5_c_im[j], s5_d[j])
            x2 = _s5_ln(x2.reshape(bsz, seq, d), wb, wc, lam, dsk, s5_wout, j, gm, bm, i
                        ).reshape(n, d)
            x2 = _ffn_ln(x2, *ffn2, i)
    return x2.reshape(bsz, seq, d)
```

```python
import functools

import jax
import jax.numpy as jnp
import numpy as np
from jax import lax
from jax.experimental import pallas as pl
from jax.experimental.pallas import tpu as pltpu

F32 = jnp.float32
BF16 = jnp.bfloat16

D_MODEL = 1024
DEPTH = 4
FOX_HEADS = 16
HEAD_DIM = D_MODEL // FOX_HEADS
S5_GROUP = 16
S5_GROUPS = D_MODEL // S5_GROUP
S5_STATE = 64
D_FF = 2816
ALPHA = (2.0 * DEPTH) ** 0.25
LN_EPS = 1e-5
NEG_INF = -1e30

LANES = 128
MXU_DIM = 256
VMEM_LIMIT = 56 * 1024 * 1024

FFN_TM = 1024
FFN_PARTS = 4
FFN_PROJ_PARTS = 2
FFN_FC = 256
TOK_T = 512
ATT_T = 512
S5_PITCH = TOK_T + 8
S5_KT = MXU_DIM
S5_NKT = D_MODEL // S5_KT
S5_SL = (S5_KT // S5_GROUP) * S5_STATE
C_PARTS = 3
DEN_ROWS = 16
LOG2E = 1.4426950408889634
HEAD_SLOT_SHIFT = 3
HEAD_SLOT = 1 << HEAD_SLOT_SHIFT


def _params(n_grid):
    return pltpu.CompilerParams(dimension_semantics=("arbitrary",) * n_grid,
                                vmem_limit_bytes=VMEM_LIMIT)


def _const_spec(shape):
    nd = len(shape)
    return pl.BlockSpec(shape, lambda *_: (0,) * nd, pipeline_mode=pl.Buffered(1))


def _layer_spec(shape, layer):
    nd = len(shape) - 1
    return pl.BlockSpec((None,) + tuple(shape[1:]), lambda *_: (layer,) + (0,) * nd,
                        pipeline_mode=pl.Buffered(1))


def _ln(y, g, b):
    mu = jnp.mean(y, axis=-1, keepdims=True)
    d = y - mu
    var = jnp.mean(d * d, axis=-1, keepdims=True)
    return d * lax.rsqrt(var + LN_EPS) * g + b


def _split3(c):
    hi = c.astype(BF16)
    r1 = c - hi.astype(F32)
    mid = r1.astype(BF16)
    lo = (r1 - mid.astype(F32)).astype(BF16)
    return hi, mid, lo


def _ffn_ln_kernel(*refs, with_proj):
    if with_proj:
        x_ref, a_ref, wo_ref, gm_ref, bm_ref, win_ref, wout_ref, g_ref, b_ref, o_ref, h_ref = refs
    else:
        x_ref, win_ref, wout_ref, g_ref, b_ref, o_ref, h_ref = refs
    n_parts = FFN_PROJ_PARTS if with_proj else FFN_PARTS
    part_rows = x_ref.shape[0] // n_parts
    for part in range(n_parts):
        rows = slice(part * part_rows, (part + 1) * part_rows)
        x = x_ref[rows, :]
        if with_proj:
            m = jnp.dot(a_ref[rows, :], wo_ref[...], preferred_element_type=F32)
            x = _ln(ALPHA * x + m, gm_ref[...], bm_ref[...])
        xb = x.astype(BF16)
        for c in range(D_FF // FFN_FC):
            lo = c * FFN_FC
            gate = jnp.dot(xb, win_ref[:, lo:lo + FFN_FC], preferred_element_type=F32)
            up = jnp.dot(xb, win_ref[:, D_FF + lo:D_FF + lo + FFN_FC], preferred_element_type=F32)
            h_ref[rows, lo:lo + FFN_FC] = (gate * jax.nn.sigmoid(gate) * up).astype(BF16)
        acc = jnp.dot(h_ref[rows, :], wout_ref[...], preferred_element_type=F32)
        o_ref[rows, :] = _ln(ALPHA * x + 0.5 * acc, g_ref[...], b_ref[...])


def _ffn_ln(x2, win, wout, g, b, layer, attn=None):
    n, d = x2.shape
    tm = min(FFN_TM, n)
    tile = pl.BlockSpec((tm, d), lambda i: (i, 0))
    args, specs = [x2], [tile]
    if attn is not None:
        a2, wo, gm, bm, j = attn
        args += [a2, wo, gm, bm]
        specs += [tile, _layer_spec(wo.shape, j), _layer_spec(gm.shape, layer),
                  _layer_spec(bm.shape, layer)]
    args += [win, wout, g, b]
    specs += [_layer_spec(a.shape, layer) for a in (win, wout, g, b)]
    return pl.pallas_call(
        functools.partial(_ffn_ln_kernel, with_proj=attn is not None),
        grid=(n // tm,),
        in_specs=specs,
        out_specs=tile,
        out_shape=jax.ShapeDtypeStruct((n, d), F32),
        scratch_shapes=[pltpu.VMEM((tm, D_FF), BF16)],
        compiler_params=_params(1),
        name="ffn_ln",
    )(*args)


def _fox_proj_kernel(x_ref, wqt_ref, wk_ref, wvt_ref, wf_ref, bf_ref, eqt_ref, ek_ref, oq_ref, ok_ref,
                     qt_ref, k_ref, vt_ref, cqt_ref, ck_ref, carry_ref):
    @pl.when(pl.program_id(1) == 0)
    def _():
        carry_ref[...] = jnp.zeros_like(carry_ref)

    xb = x_ref[0].astype(BF16)
    t = xb.shape[0]
    d = D_MODEL
    k_ref[0] = jnp.dot(xb, wk_ref[...], preferred_element_type=F32).astype(BF16)
    nt = (((1,), (1,)), ((), ()))
    qt_ref[0, 0] = lax.dot_general(wqt_ref[...], xb, nt, preferred_element_type=F32).astype(BF16)
    vt_ref[0, 0] = lax.dot_general(wvt_ref[...], xb, nt, preferred_element_type=F32).astype(BF16)

    f = jnp.dot(xb, wf_ref[...], preferred_element_type=F32) + bf_ref[...]
    logf = jnp.minimum(f, 0.0) - jnp.log1p(jnp.exp(-jnp.abs(f)))
    row = lax.broadcasted_iota(jnp.int32, (t, t), 0)
    col = lax.broadcasted_iota(jnp.int32, (t, t), 1)
    tri = (col <= row).astype(BF16)
    pieces = jnp.concatenate(_split3(logf), axis=1)
    cum3 = jnp.dot(tri, pieces, preferred_element_type=F32)
    c = (cum3[:, 0:LANES] + cum3[:, LANES:2 * LANES] + cum3[:, 2 * LANES:3 * LANES]
         + carry_ref[...])
    carry_ref[...] = c[t - 1:t, :]
    ccat = jnp.concatenate(_split3(c * LOG2E), axis=1)
    cqt_ref[0, 0] = (lax.dot_general(eqt_ref[...], ccat, nt, preferred_element_type=F32)
                     + oq_ref[...]).astype(BF16)
    ck_ref[0] = (jnp.dot(ccat, ek_ref[...], preferred_element_type=F32) + ok_ref[...]).astype(BF16)


def _bias_routing():
    eq = np.zeros((C_PARTS * LANES, LANES), np.float32)
    ek = np.zeros((C_PARTS * LANES, LANES), np.float32)
    oq = np.zeros((1, LANES), np.float32)
    ok = np.zeros((1, LANES), np.float32)
    h = np.arange(FOX_HEADS)
    for p in range(C_PARTS):
        eq[p * LANES + h, h * HEAD_SLOT + p] = 1.0
        ek[p * LANES + h, h * HEAD_SLOT + C_PARTS + p] = -1.0
        oq[0, h * HEAD_SLOT + C_PARTS + p] = 1.0
        ok[0, h * HEAD_SLOT + p] = 1.0
    return (jnp.asarray(eq.T, BF16), jnp.asarray(ek, BF16), jnp.asarray(oq.T), jnp.asarray(ok))


def _fox_proj(x3, wqt, wk, wvt, wf, bf):
    bsz, seq, d = x3.shape
    t = min(ATT_T, seq)
    eqt, ek, oq, ok = _bias_routing()
    tok = lambda w: pl.BlockSpec((1, t, w), lambda b, i: (b, i, 0))
    tr = lambda w: pl.BlockSpec((1, 1, w, t), lambda b, i: (b, i, 0, 0))
    big = jax.ShapeDtypeStruct((bsz, seq, d), BF16)
    big_t = jax.ShapeDtypeStruct((bsz, seq // t, d, t), BF16)
    small = jax.ShapeDtypeStruct((bsz, seq, LANES), BF16)
    return pl.pallas_call(
        _fox_proj_kernel,
        grid=(bsz, seq // t),
        in_specs=[tok(d), _const_spec(wqt.shape), _const_spec(wk.shape), _const_spec(wvt.shape),
                  _const_spec(wf.shape), _const_spec(bf.shape), _const_spec(eqt.shape),
                  _const_spec(ek.shape), _const_spec(oq.shape), _const_spec(ok.shape)],
        out_specs=[tr(d), tok(d), tr(d), tr(LANES), tok(LANES)],
        out_shape=[big_t, big, big_t, jax.ShapeDtypeStruct((bsz, seq // t, LANES, t), BF16), small],
        scratch_shapes=[pltpu.VMEM((1, LANES), F32)],
        compiler_params=_params(2),
        name="fox_proj",
    )(x3, wqt, wk, wvt, wf, bf, eqt, ek, oq, ok)


def _fox_attn_kernel(qt_ref, cqt_ref, k_ref, ck_ref, vt_ref, o_ref,
                     m_ref, acc_ref, sa_ref, sb_ref):
    pair = pl.program_id(1)
    n_q, _, t = qt_ref.shape[1:]
    feat = lax.broadcasted_iota(jnp.int32, (LANES, t), 0)
    key_pos = lax.broadcasted_iota(jnp.int32, (t, t), 0)
    qry_pos = lax.broadcasted_iota(jnp.int32, (t, t), 1)
    first_half = feat < HEAD_DIM
    ones_rows = jnp.ones((DEN_ROWS, t), BF16)

    def scores(j, lhs_t, s_ref):
        off = pl.multiple_of(j * t, t)
        rhs = jnp.concatenate([k_ref[0, pl.ds(off, t), :], ck_ref[0, pl.ds(off, t), :]], axis=1)
        for hh in range(2):
            s_ref[hh] = jnp.dot(rhs, lhs_t[hh], preferred_element_type=F32)

    def absorb(j, s_ref, masked):
        for hh in range(2):
            st = s_ref[hh]
            if masked:
                st = jnp.where(key_pos <= qry_pos, st, NEG_INF)
            m_prev = m_ref[hh]
            m_new = jnp.maximum(m_prev, jnp.max(st, axis=0, keepdims=True))
            pt = jnp.exp2(st - m_new)
            a = jnp.exp2(m_prev - m_new)
            vt = jnp.concatenate([vt_ref[0, j, hh * HEAD_DIM:(hh + 1) * HEAD_DIM, :], ones_rows],
                                 axis=0)
            acc_ref[hh] = a * acc_ref[hh] + jnp.dot(vt, pt.astype(BF16),
                                                    preferred_element_type=F32)
            m_ref[hh] = m_new

    for qi in range(n_q):
        qt = qt_ref[0, qi]
        cqt = cqt_ref[0, qi]
        lhs_t = []
        for hh in range(2):
            head = 2 * pair + hh
            qm = jnp.where(first_half if hh == 0 else jnp.logical_not(first_half), qt,
                           jnp.zeros_like(qt))
            cm = jnp.where((feat >> HEAD_SLOT_SHIFT) == head, cqt, jnp.zeros_like(cqt))
            lhs_t.append(jnp.concatenate([qm, cm], axis=0))
        m_ref[...] = jnp.full(m_ref.shape, NEG_INF, F32)
        acc_ref[...] = jnp.zeros(acc_ref.shape, F32)

        scores(0, lhs_t, sa_ref)

        def body(i, carry, lhs_t=lhs_t):
            j = 2 * i
            scores(j + 1, lhs_t, sb_ref)
            absorb(j, sa_ref, False)
            scores(j + 2, lhs_t, sa_ref)
            absorb(j + 1, sb_ref, False)
            return carry

        for i in range(qi // 2):
            body(i, 0)
        if qi % 2 == 0:
            absorb(qi, sa_ref, True)
        else:
            scores(qi, lhs_t, sb_ref)
            absorb(qi - 1, sa_ref, False)
            absorb(qi, sb_ref, True)

        ot = jnp.concatenate([acc_ref[hh, :HEAD_DIM, :] / acc_ref[hh, HEAD_DIM:HEAD_DIM + 1, :]
                              for hh in range(2)], axis=0)
        o_ref[0, qi * t:(qi + 1) * t, :] = ot.T.astype(BF16)


def _fox_attn(qt, cqt, k, ck, vt):
    bsz, seq, d = k.shape
    t = min(ATT_T, seq)
    n_q = seq // t
    return pl.pallas_call(
        _fox_attn_kernel,
        grid=(bsz, d // LANES),
        in_specs=[pl.BlockSpec((1, n_q, LANES, t), lambda b, p: (b, 0, p, 0)),
                  pl.BlockSpec((1, n_q, LANES, t), lambda b, p: (b, 0, 0, 0)),
                  pl.BlockSpec((1, seq, LANES), lambda b, p: (b, 0, p)),
                  pl.BlockSpec((1, seq, LANES), lambda b, p: (b, 0, 0)),
                  pl.BlockSpec((1, n_q, LANES, t), lambda b, p: (b, 0, p, 0))],
        out_specs=pl.BlockSpec((1, seq, LANES), lambda b, p: (b, 0, p)),
        out_shape=jax.ShapeDtypeStruct((bsz, seq, d), BF16),
        scratch_shapes=[pltpu.VMEM((2, 1, t), F32),
                        pltpu.VMEM((2, HEAD_DIM + DEN_ROWS, t), F32),
                        pltpu.VMEM((2, t, t), F32), pltpu.VMEM((2, t, t), F32)],
        compiler_params=_params(2),
        name="fox_attn",
    )(qt, cqt, k, ck, vt)


def _cmul(ar, ai, br, bi):
    return ar * br - ai * bi, ar * bi + ai * br


def _s5_ln_kernel(x_ref, wb_ref, wc_ref, lam_ref, d_ref, wout_ref, g_ref, b_ref, o_ref,
                  sa_ref, sb_ref, y_ref, carry_ref):
    @pl.when(pl.program_id(1) == 0)
    def _():
        carry_ref[...] = jnp.zeros_like(carry_ref)

    x = x_ref[0]
    xb = x.astype(BF16)
    t = x.shape[0]
    nb = S5_SL // LANES
    per_half = S5_NKT // 2
    slab = lambda j: slice(j * S5_PITCH, j * S5_PITCH + t)
    halves = [(sa_ref, range(0, per_half)), (sb_ref, range(per_half, S5_NKT))]

    def project_in(s_ref, kts):
        for i, kt in enumerate(kts):
            cols = slice(kt * S5_KT, (kt + 1) * S5_KT)
            bu = jnp.dot(xb[:, cols], wb_ref[kt], preferred_element_type=F32)
            for j in range(nb):
                s_ref[2 * i, slab(j), :] = bu[:, j * LANES:(j + 1) * LANES]
                s_ref[2 * i + 1, slab(j), :] = bu[:, S5_SL + j * LANES:S5_SL + (j + 1) * LANES]

    def recur(s_ref, kts):
        rows = lambda kt, part: slice((2 * kt + part) * nb, (2 * kt + part + 1) * nb)
        lam = [(lam_ref[rows(kt, 0), :], lam_ref[rows(kt, 1), :]) for kt in kts]
        h = [(carry_ref[rows(kt, 0), :], carry_ref[rows(kt, 1), :]) for kt in kts]
        for ti in range(t):
            now = pl.ds(ti, nb, stride=S5_PITCH)
            for i in range(len(h)):
                mr, mi = _cmul(lam[i][0], lam[i][1], h[i][0], h[i][1])
                hr = mr + s_ref[2 * i, now, :]
                hi = mi + s_ref[2 * i + 1, now, :]
                s_ref[2 * i, now, :] = hr
                s_ref[2 * i + 1, now, :] = hi
                h[i] = (hr, hi)
        for i, kt in enumerate(kts):
            carry_ref[rows(kt, 0), :] = h[i][0]
            carry_ref[rows(kt, 1), :] = h[i][1]

    def project_out(s_ref, kts):
        for i, kt in enumerate(kts):
            cols = slice(kt * S5_KT, (kt + 1) * S5_KT)
            hcat = jnp.concatenate([s_ref[2 * i + part, slab(j), :].astype(BF16)
                                    for part in range(2) for j in range(nb)], axis=1)
            y_ref[:, cols] = (jnp.dot(hcat, wc_ref[kt], preferred_element_type=F32)
                              + d_ref[:, cols] * x[:, cols])

    (s_a, kts_a), (s_b, kts_b) = halves
    project_in(s_a, kts_a)
    project_in(s_b, kts_b)
    recur(s_a, kts_a)

    @pl.when(pl.program_id(1) >= 0)
    def _():
        recur(s_b, kts_b)
        project_out(s_a, kts_a)

    project_out(s_b, kts_b)
    ge = jax.nn.gelu(y_ref[...]).astype(BF16)
    vg = jnp.dot(ge, wout_ref[...], preferred_element_type=F32)
    m = vg[:, :D_MODEL] * jax.nn.sigmoid(vg[:, D_MODEL:])
    o_ref[0] = _ln(ALPHA * x + m, g_ref[...], b_ref[...])


def _s5_tables(a_re, a_im, log_dt, b_re, b_im, c_re, c_im, d_skip):
    lr = a_re.astype(F32)
    li = a_im.astype(F32)
    dt = jnp.exp(log_dt.astype(F32))[:, None]
    mag = jnp.exp(lr * dt)
    ang = li * dt
    lb_re = mag * jnp.cos(ang)
    lb_im = mag * jnp.sin(ang)
    den = lr * lr + li * li
    nr = lb_re - 1.0
    ni = lb_im
    z_re = (nr * lr + ni * li) / den
    z_im = (ni * lr - nr * li) / den
    br = b_re.astype(F32)
    bi = b_im.astype(F32)
    bb_re = z_re[..., None] * br - z_im[..., None] * bi
    bb_im = z_re[..., None] * bi + z_im[..., None] * br
    gk = S5_KT // S5_GROUP

    def blockdiag(m, width):
        a, b = m.shape[1:]
        tiled = jnp.tile(m.reshape(S5_NKT, gk, a, b), (1, 1, 1, gk))
        on_diag = (np.arange(gk * b) // b)[None, :] == np.arange(gk)[:, None]
        return jnp.where(on_diag[None, :, None, :], tiled, 0.0).reshape(S5_NKT, gk * a, width)

    blockdiag_b = lambda m: blockdiag(jnp.swapaxes(m, 1, 2), S5_SL)
    blockdiag_c = lambda m: blockdiag(jnp.swapaxes(m, 1, 2), S5_KT)

    wb = jnp.concatenate([blockdiag_b(bb_re), blockdiag_b(bb_im)], axis=2).astype(BF16)
    wc = jnp.concatenate([blockdiag_c(c_re.astype(F32)), -blockdiag_c(c_im.astype(F32))],
                         axis=1).astype(BF16)
    lam = jnp.stack([lb_re.reshape(S5_NKT, S5_SL), lb_im.reshape(S5_NKT, S5_SL)], axis=1)
    lam = lam.reshape(2 * S5_NKT * (S5_SL // LANES), LANES)
    return wb, wc, lam, d_skip.astype(F32).reshape(1, D_MODEL)


def _s5_ln(x3, wb, wc, lam, dsk, wout, j, g, b, layer):
    bsz, seq, d = x3.shape
    t = min(TOK_T, seq)
    tok = pl.BlockSpec((1, t, d), lambda bi, i: (bi, i, 0))
    return pl.pallas_call(
        _s5_ln_kernel,
        grid=(bsz, seq // t),
        in_specs=[tok, _const_spec(wb.shape), _const_spec(wc.shape), _const_spec(lam.shape),
                  _const_spec(dsk.shape), _layer_spec(wout.shape, j), _layer_spec(g.shape, layer),
                  _layer_spec(b.shape, layer)],
        out_specs=tok,
        out_shape=jax.ShapeDtypeStruct((bsz, seq, d), F32),
        scratch_shapes=[pltpu.VMEM((S5_NKT, (S5_SL // LANES) * S5_PITCH, LANES), F32),
                        pltpu.VMEM((S5_NKT, (S5_SL // LANES) * S5_PITCH, LANES), F32),
                        pltpu.VMEM((t, d), F32), pltpu.VMEM(lam.shape, F32)],
        compiler_params=_params(2),
        name="s5_ln",
    )(x3, wb, wc, lam, dsk, wout, g, b)


def kernel(x, ffn1_w_in, ffn1_w_out, ln1_g, ln1_b, lnm_g, lnm_b, ffn2_w_in, ffn2_w_out, ln2_g, ln2_b, fox_w_in, fox_b_f, fox_w_o, s5_a_re, s5_a_im, s5_log_dt, s5_b_re, s5_b_im, s5_c_re, s5_c_im, s5_d, s5_w_out):
    bsz, seq, d = x.shape
    n = bsz * seq
    rows = lambda v: v.astype(F32).reshape(v.shape[0], 1, d)
    ffn1 = (ffn1_w_in.astype(BF16), ffn1_w_out.astype(BF16), rows(ln1_g), rows(ln1_b))
    ffn2 = (ffn2_w_in.astype(BF16), ffn2_w_out.astype(BF16), rows(ln2_g), rows(ln2_b))
    gm, bm = rows(lnm_g), rows(lnm_b)
    fox_wo = fox_w_o.astype(BF16)
    s5_wout = s5_w_out.astype(BF16)
    x2 = x.reshape(n, d)
    for i in range(DEPTH):
        x2 = _ffn_ln(x2, *ffn1, i)
        j = i // 2
        if i % 2 == 0:
            w = fox_w_in[j]
            scale = LOG2E / (HEAD_DIM ** 0.5)
            wqt = (w[:, :d] * scale).T.astype(BF16)
            wk = w[:, d:2 * d].astype(BF16)
            wvt = w[:, 2 * d:3 * d].T.astype(BF16)
            wf = jnp.pad(w[:, 3 * d:], ((0, 0), (0, LANES - FOX_HEADS))).astype(BF16)
            bf = jnp.pad(fox_b_f[j].astype(F32), (0, LANES - FOX_HEADS)).reshape(1, LANES)
            qt, k, vt, cqt, ck = _fox_proj(x2.reshape(bsz, seq, d), wqt, wk, wvt, wf, bf)
            a = _fox_attn(qt, cqt, k, ck, vt)
            x2 = _ffn_ln(x2, *ffn2, i, attn=(a.reshape(n, d), fox_wo, gm, bm, j))
        else:
            wb, wc, lam, dsk = _s5_tables(s5_a_re[j], s5_a_im[j], s5_log_dt[j], s5_b_re[j],
                                          s5_b_im[j], s5_c_re[j], s5_c_im[j], s5_d[j])
            x2 = _s5_ln(x2.reshape(bsz, seq, d), wb, wc, lam, dsk, s5_wout, j, gm, bm, i
                        ).reshape(n, d)
            x2 = _ffn_ln(x2, *ffn2, i)
    return x2.reshape(bsz, seq, d)
```
